```python
import jax, jax.numpy as jnp
from jax import lax
import numpy as np

D_MODEL = 2048
BATCH = 16
SEQ = 2048
DEPTH = 4
DEC_BATCH = 32
DEC_SEQ = 32
PAST_LEN = 1024

CHUNK = 64
N_MIXERS = 3
N_POOL_LAYERS = (DEPTH + 2) // 3
N_SCONV_LAYERS = (DEPTH + 1) // 3
N_CCONV_LAYERS = DEPTH // 3
POOL_WINDOWS = (2, 4, 8, 16)
N_POOL_GROUPS = len(POOL_WINDOWS)
POOL_GROUP = D_MODEL // N_POOL_GROUPS
POOL_HIST = max(POOL_WINDOWS) - 1
SCONV_WIDTH = 3
CCONV_WIDTH = 31
D_FF = 5632
N_MEM = 256
N_MEM_HEADS = 4
MEM_HEAD_DIM = D_MODEL // N_MEM_HEADS
EPS = 1e-6
MACARON = 0.5

kernel_name = "hybrid_streaming_pool_conv_conformer_step"


def rmsnorm(x, g):
    x32 = x.astype(jnp.float32)
    y = x32 * lax.rsqrt(jnp.mean(x32 * x32, axis=-1, keepdims=True) + EPS)
    return (y * g.astype(jnp.float32)).astype(x.dtype)


def layernorm(x, g, b):
    x32 = x.astype(jnp.float32)
    xc = x32 - jnp.mean(x32, axis=-1, keepdims=True)
    y = xc * lax.rsqrt(jnp.mean(xc * xc, axis=-1, keepdims=True) + EPS)
    return (y * g.astype(jnp.float32) + b.astype(jnp.float32)).astype(x.dtype)


def swiglu(u, w_gu, w_down):
    gate, up = jnp.split(u @ w_gu, 2, axis=-1)
    return (jax.nn.silu(gate) * up) @ w_down


def causal_depthwise(ext, w):
    return lax.conv_general_dilated(ext, w[:, None, :].astype(ext.dtype), window_strides=(1,), padding='VALID',
                                    dimension_numbers=('NWC', 'WIO', 'NWC'),
                                    feature_group_count=ext.shape[-1])


def pool_mixer(ext, pos0, w_groups, scale):
    B, L, D = ext.shape
    T = L - POOL_HIST
    e32 = ext.astype(jnp.float32)
    cs = jnp.concatenate([jnp.zeros((B, 1, D), jnp.float32), jnp.cumsum(e32, axis=1)], axis=1)
    end = cs[:, POOL_HIST + 1:]
    u = e32[:, POOL_HIST:]
    pos = pos0 + jnp.arange(T)
    groups = []
    for g, w in enumerate(POOL_WINDOWS):
        sl = slice(g * POOL_GROUP, (g + 1) * POOL_GROUP)
        start = cs[:, POOL_HIST + 1 - w: POOL_HIST + 1 - w + T, sl]
        cnt = jnp.minimum(pos + 1, w).astype(jnp.float32)[None, :, None]
        groups.append((end[..., sl] - start) / cnt - u[..., sl])
    d = jnp.stack(groups, axis=2).astype(ext.dtype)
    y = jnp.einsum('btgc,gcd->btgd', d, w_groups).reshape(B, T, D)
    return y * scale


def sconv_mixer(u, hist, w_in, w_conv, w_out):
    gb, gc, h = jnp.split(u @ w_in, 3, axis=-1)
    v = gc * h
    ext = jnp.concatenate([hist.astype(v.dtype), v], axis=1)
    y = gb * causal_depthwise(ext, w_conv)
    return y @ w_out, ext[:, -(SCONV_WIDTH - 1):]


def cconv_mixer(u, hist, w_pw1, b_pw1, w_dw, b_dw, ln_g, ln_b, w_pw2, b_pw2):
    a, gate = jnp.split(u @ w_pw1 + b_pw1, 2, axis=-1)
    v = a * jax.nn.sigmoid(gate)
    ext = jnp.concatenate([hist.astype(v.dtype), v], axis=1)
    h = causal_depthwise(ext, w_dw) + b_dw
    h = jax.nn.silu(layernorm(h, ln_g, ln_b))
    return h @ w_pw2 + b_pw2, ext[:, -(CCONV_WIDTH - 1):]


def mem_kv(mem, g_mem, w_k, w_v):
    B = mem.shape[0]
    m = rmsnorm(mem, g_mem)
    k = (m @ w_k).reshape(B, N_MEM, N_MEM_HEADS, MEM_HEAD_DIM)
    v = (m @ w_v).reshape(B, N_MEM, N_MEM_HEADS, MEM_HEAD_DIM)
    return k, v


def cross_attend(u, k, v, w_q, w_o):
    B, T, D = u.shape
    q = (u @ w_q).reshape(B, T, N_MEM_HEADS, MEM_HEAD_DIM)
    s = jnp.einsum('bthd,bmhd->bhtm', q, k.astype(q.dtype)).astype(jnp.float32) * (MEM_HEAD_DIM ** -0.5)
    p = jax.nn.softmax(s, axis=-1).astype(q.dtype)
    o = jnp.einsum('bhtm,bmhd->bthd', p, v.astype(q.dtype)).reshape(B, T, D)
    return o @ w_o


def setup_inputs(seed: int = 0) -> dict:
    key = jax.random.key(seed)
    ks = iter(jax.random.split(key, 40))
    f32 = jnp.float32

    def nrm(shape, scale=1.0):
        return jax.random.normal(next(ks), shape, f32) * scale

    def gain(shape):
        return 1.0 + nrm(shape, 0.1)

    D = D_MODEL
    return {
        "x_prompt": nrm((BATCH, SEQ, D)),
        "x_sample": nrm((DEC_BATCH, DEC_SEQ, D)),
        "mem_prompt": nrm((BATCH, N_MEM, D)),
        "cache_pool": nrm((N_POOL_LAYERS, DEC_BATCH, POOL_HIST, D)),
        "cache_sconv": nrm((N_SCONV_LAYERS, DEC_BATCH, SCONV_WIDTH - 1, D)),
        "cache_cconv": nrm((N_CCONV_LAYERS, DEC_BATCH, CCONV_WIDTH - 1, D), 0.5),
        "cache_mem_k": nrm((DEPTH, DEC_BATCH, N_MEM, N_MEM_HEADS, MEM_HEAD_DIM)),
        "cache_mem_v": nrm((DEPTH, DEC_BATCH, N_MEM, N_MEM_HEADS, MEM_HEAD_DIM)),
        "g_pre": gain((DEPTH, 4, D)),
        "g_post": gain((DEPTH, 4, D)),
        "g_mem": gain((DEPTH, D)),
        "ffn_w_gu": nrm((DEPTH, 2, D, 2 * D_FF), D ** -0.5),
        "ffn_w_down": nrm((DEPTH, 2, D_FF, D), D_FF ** -0.5),
        "xa_w_q": nrm((DEPTH, D, D), D ** -0.5),
        "xa_w_k": nrm((DEPTH, D, D), D ** -0.5),
        "xa_w_v": nrm((DEPTH, D, D), D ** -0.5),
        "xa_w_o": nrm((DEPTH, D, D), D ** -0.5),
        "pool_w": nrm((N_POOL_LAYERS, N_POOL_GROUPS, POOL_GROUP, POOL_GROUP), POOL_GROUP ** -0.5),
        "pool_scale": gain((N_POOL_LAYERS, D)),
        "sc_w_in": nrm((N_SCONV_LAYERS, D, 3 * D), D ** -0.5),
        "sc_w_conv": nrm((N_SCONV_LAYERS, SCONV_WIDTH, D), SCONV_WIDTH ** -0.5),
        "sc_w_out": nrm((N_SCONV_LAYERS, D, D), D ** -0.5),
        "cc_w_pw1": nrm((N_CCONV_LAYERS, D, 2 * D), D ** -0.5),
        "cc_b_pw1": nrm((N_CCONV_LAYERS, 2 * D), 0.02),
        "cc_w_dw": nrm((N_CCONV_LAYERS, CCONV_WIDTH, D), CCONV_WIDTH ** -0.5),
        "cc_b_dw": nrm((N_CCONV_LAYERS, D), 0.02),
        "cc_ln_g": gain((N_CCONV_LAYERS, D)),
        "cc_ln_b": nrm((N_CCONV_LAYERS, D), 0.02),
        "cc_w_pw2": nrm((N_CCONV_LAYERS, D, D), D ** -0.5),
        "cc_b_pw2": nrm((N_CCONV_LAYERS, D), 0.02),
    }


def reference(x_prompt, x_sample, mem_prompt, cache_pool, cache_sconv, cache_cconv, cache_mem_k, cache_mem_v,
              g_pre, g_post, g_mem, ffn_w_gu, ffn_w_down, xa_w_q, xa_w_k, xa_w_v, xa_w_o,
              pool_w, pool_scale, sc_w_in, sc_w_conv, sc_w_out,
              cc_w_pw1, cc_b_pw1, cc_w_dw, cc_b_dw, cc_ln_g, cc_ln_b, cc_w_pw2, cc_b_pw2):

    def trunk(x, pos0, pool_hist, sconv_hist, cconv_hist, mem_k, mem_v):
        new_pool, new_sconv, new_cconv = [], [], []
        for i in range(DEPTH):
            h = rmsnorm(x, g_pre[i, 0])
            x = x + MACARON * rmsnorm(swiglu(h, ffn_w_gu[i, 0], ffn_w_down[i, 0]), g_post[i, 0])
            h = rmsnorm(x, g_pre[i, 1])
            kind, slot = i % N_MIXERS, i // N_MIXERS
            if kind == 0:
                ext = jnp.concatenate([pool_hist[slot].astype(h.dtype), h], axis=1)
                y = pool_mixer(ext, pos0, pool_w[slot], pool_scale[slot])
                new_pool.append(ext[:, -POOL_HIST:])
            elif kind == 1:
                y, st = sconv_mixer(h, sconv_hist[slot], sc_w_in[slot], sc_w_conv[slot], sc_w_out[slot])
                new_sconv.append(st)
            else:
                y, st = cconv_mixer(h, cconv_hist[slot], cc_w_pw1[slot], cc_b_pw1[slot], cc_w_dw[slot], cc_b_dw[slot],
                                    cc_ln_g[slot], cc_ln_b[slot], cc_w_pw2[slot], cc_b_pw2[slot])
                new_cconv.append(st)
            x = x + rmsnorm(y, g_post[i, 1])
            h = rmsnorm(x, g_pre[i, 2])
            x = x + rmsnorm(cross_attend(h, mem_k[i], mem_v[i], xa_w_q[i], xa_w_o[i]), g_post[i, 2])
            h = rmsnorm(x, g_pre[i, 3])
            x = x + MACARON * rmsnorm(swiglu(h, ffn_w_gu[i, 1], ffn_w_down[i, 1]), g_post[i, 3])
        return x, jnp.stack(new_pool), jnp.stack(new_sconv), jnp.stack(new_cconv)

    D = D_MODEL
    dt = x_prompt.dtype
    kv = [mem_kv(mem_prompt, g_mem[i], xa_w_k[i], xa_w_v[i]) for i in range(DEPTH)]
    mem_k_p = [k for k, _ in kv]
    mem_v_p = [v for _, v in kv]
    pool_h0 = [jnp.zeros((BATCH, POOL_HIST, D), dt)] * N_POOL_LAYERS
    sconv_h0 = [jnp.zeros((BATCH, SCONV_WIDTH - 1, D), dt)] * N_SCONV_LAYERS
    cconv_h0 = [jnp.zeros((BATCH, CCONV_WIDTH - 1, D), dt)] * N_CCONV_LAYERS
    y_prompt, pool_p, sconv_p, cconv_p = trunk(x_prompt, 0, pool_h0, sconv_h0, cconv_h0, mem_k_p, mem_v_p)

    y_sample, pool_s, sconv_s, cconv_s = trunk(
        x_sample, PAST_LEN,
        [cache_pool[j] for j in range(N_POOL_LAYERS)],
        [cache_sconv[j] for j in range(N_SCONV_LAYERS)],
        [cache_cconv[j] for j in range(N_CCONV_LAYERS)],
        [cache_mem_k[i] for i in range(DEPTH)],
        [cache_mem_v[i] for i in range(DEPTH)])

    mem_k_prompt = jnp.stack(mem_k_p)
    mem_v_prompt = jnp.stack(mem_v_p)
    return (y_prompt, y_sample, pool_p, pool_s, sconv_p, sconv_s, cconv_p, cconv_s, mem_k_prompt, mem_v_prompt)
```

```python
import functools

import jax
import jax.numpy as jnp
from jax import lax
from jax.experimental import pallas as pl
from jax.experimental.pallas import tpu as pltpu

EPS = 1e-6
MACARON = 0.5
PAST_LEN = 1024
POOL_WINDOWS = (2, 4, 8, 16)
POOL_HIST = max(POOL_WINDOWS) - 1
N_MIXERS = 3

BF16 = jnp.bfloat16
F32 = jnp.float32

V7X_VMEM_BYTES = 64 * 1024 * 1024
SUBLANES = 8

ROW_TILE = 512
COL_CHUNK = 512


def _vmem_limit(block_bytes, scratch_bytes, temp_bytes):
    est = 2 * block_bytes + scratch_bytes + temp_bytes
    return int(min(est + est // 8, V7X_VMEM_BYTES - 6 * 1024 * 1024))


def _nbytes(shape, dtype):
    n = 1
    for s in shape:
        n *= s
    return n * jnp.dtype(dtype).itemsize


def _rms(x, g):
    ms = jnp.mean(x * x, axis=-1, keepdims=True)
    return x * lax.rsqrt(ms + EPS) * g


def _dot(a, b):
    return jnp.dot(a, b, preferred_element_type=F32)


def _hist_pad(h):
    return -(-h // SUBLANES) * SUBLANES


def _ffn_kernel(x_ref, gpre_ref, gpost_ref, wg_ref, wu_ref, wd_ref, o_ref, h_ref):
    j = pl.program_id(1)
    nj = pl.num_programs(1)

    @pl.when(j == 0)
    def _():
        h_ref[...] = _rms(x_ref[...], gpre_ref[...]).astype(BF16)

    h = h_ref[...]
    gate = _dot(h, wg_ref[...])
    up = _dot(h, wu_ref[...])
    a = (gate * jax.nn.sigmoid(gate) * up).astype(BF16)

    @pl.when(j == 0)
    def _():
        o_ref[...] = _dot(a, wd_ref[...])

    @pl.when(j > 0)
    def _():
        o_ref[...] += _dot(a, wd_ref[...])

    @pl.when(j == nj - 1)
    def _():
        o_ref[...] = x_ref[...] + MACARON * _rms(o_ref[...], gpost_ref[...])


def _ffn(x, g_pre, g_post, w_gu, w_down, layer, slot, gidx):
    n, d = x.shape
    dff = w_down.shape[2]
    tm = min(ROW_TILE, n)
    tf = min(COL_CHUNK, dff)
    nj = dff // tf
    assert n % tm == 0 and dff % tf == 0
    blocks = (_nbytes((tm, d), F32) * 2 + _nbytes((d, tf), BF16) * 3)
    scratch = _nbytes((tm, d), BF16)
    temps = _nbytes((tm, tf), F32) * 4 + _nbytes((tm, d), F32)
    return pl.pallas_call(
        _ffn_kernel,
        grid=(n // tm, nj),
        in_specs=[
            pl.BlockSpec((tm, d), lambda i, j: (i, 0)),
            pl.BlockSpec((None, 1, d), lambda i, j: (gidx, 0, 0)),
            pl.BlockSpec((None, 1, d), lambda i, j: (gidx, 0, 0)),
            pl.BlockSpec((None, None, d, tf), lambda i, j: (layer, slot, 0, j)),
            pl.BlockSpec((None, None, d, tf), lambda i, j: (layer, slot, 0, nj + j)),
            pl.BlockSpec((None, None, tf, d), lambda i, j: (layer, slot, j, 0)),
        ],
        out_specs=pl.BlockSpec((tm, d), lambda i, j: (i, 0)),
        out_shape=jax.ShapeDtypeStruct((n, d), F32),
        scratch_shapes=[pltpu.VMEM((tm, d), BF16)],
        compiler_params=pltpu.CompilerParams(
            dimension_semantics=("parallel", "arbitrary"),
            vmem_limit_bytes=_vmem_limit(blocks, scratch, temps)),
        name="ffn",
    )(x, g_pre, g_post, w_gu, w_gu, w_down)


def _seq_tiles(b, t, rows=None):
    rows = ROW_TILE if rows is None else rows
    if t >= rows:
        assert t % rows == 0
        return 1, rows
    nb = min(b, max(1, rows // t))
    assert b % nb == 0 and t % SUBLANES == 0
    return nb, t


def _pool_kernel(x_ref, hist_ref, gpre_ref, gpost_ref, w_ref, scale_ref,
                 o_ref, nh_ref, ext_ref, *, pos0):
    t = pl.program_id(1)
    nt = pl.num_programs(1)
    nb, tt, d = x_ref.shape
    hp = _hist_pad(POOL_HIST)
    gc = d // len(POOL_WINDOWS)

    @pl.when(t == 0)
    def _():
        ext_ref[:, hp - POOL_HIST:hp, :] = hist_ref[...]

    @pl.when(t > 0)
    def _():
        ext_ref[:, hp - POOL_HIST:hp, :] = ext_ref[:, hp + tt - POOL_HIST:hp + tt, :]

    x = x_ref[...]
    ext_ref[:, hp:hp + tt, :] = _rms(x, gpre_ref[...])
    pos = pos0 + t * tt + lax.broadcasted_iota(jnp.int32, (1, tt, 1), 1)
    ys = []
    for g, w in enumerate(POOL_WINDOWS):
        sl = slice(g * gc, (g + 1) * gc)
        u = ext_ref[:, hp:hp + tt, sl]
        win = u
        for k in range(1, w):
            win = win + ext_ref[:, hp - k:hp - k + tt, sl]
        cnt = jnp.minimum(pos + 1, w).astype(F32)
        dg = (win / cnt - u).reshape(nb * tt, gc).astype(BF16)
        ys.append(_dot(dg, w_ref[g]))
    y = jnp.concatenate(ys, axis=-1) * scale_ref[...]
    out = x.reshape(nb * tt, d) + _rms(y, gpost_ref[...])
    o_ref[...] = out.reshape(nb, tt, d)

    @pl.when(t == nt - 1)
    def _():
        nh_ref[...] = ext_ref[:, hp + tt - POOL_HIST:hp + tt, :]


def _pool_mixer(x, hist, g_pre, g_post, pool_w, pool_scale, slot, gidx, pos0):
    b, t, d = x.shape
    nb, tt = _seq_tiles(b, t)
    assert tt >= POOL_HIST
    ng = len(POOL_WINDOWS)
    gc = d // ng
    hp = _hist_pad(POOL_HIST)
    blocks = _nbytes((nb, tt, d), F32) * 2 + _nbytes((ng, gc, gc), BF16)
    scratch = _nbytes((nb, hp + tt, d), F32)
    temps = _nbytes((nb, tt, d), F32) * 4
    return pl.pallas_call(
        functools.partial(_pool_kernel, pos0=pos0),
        grid=(b // nb, t // tt),
        in_specs=[
            pl.BlockSpec((nb, tt, d), lambda i, j: (i, j, 0)),
            pl.BlockSpec((nb, POOL_HIST, d), lambda i, j: (i, 0, 0)),
            pl.BlockSpec((None, 1, d), lambda i, j: (gidx, 0, 0)),
            pl.BlockSpec((None, 1, d), lambda i, j: (gidx, 0, 0)),
            pl.BlockSpec((None, ng, gc, gc), lambda i, j: (slot, 0, 0, 0)),
            pl.BlockSpec((None, 1, d), lambda i, j: (slot, 0, 0)),
        ],
        out_specs=[
            pl.BlockSpec((nb, tt, d), lambda i, j: (i, j, 0)),
            pl.BlockSpec((nb, POOL_HIST, d), lambda i, j: (i, 0, 0)),
        ],
        out_shape=[jax.ShapeDtypeStruct((b, t, d), F32),
                   jax.ShapeDtypeStruct((b, POOL_HIST, d), F32)],
        scratch_shapes=[pltpu.VMEM((nb, hp + tt, d), F32)],
        compiler_params=pltpu.CompilerParams(
            dimension_semantics=("parallel", "arbitrary"),
            vmem_limit_bytes=_vmem_limit(blocks, scratch, temps)),
        name="pool_mixer",
    )(x, hist, g_pre, g_post, pool_w, pool_scale)


def _sconv_kernel(x_ref, hist_ref, gpre_ref, gpost_ref, wb_ref, wc_ref, wh_ref,
                  wconv_ref, wout_ref, o_ref, nh_ref, u_ref, carry_ref, ext_ref):
    t = pl.program_id(1)
    c = pl.program_id(2)
    nc = pl.num_programs(2)
    nb, tt, d = x_ref.shape
    tn = wb_ref.shape[-1]
    width = wconv_ref.shape[0]
    nh = width - 1
    hp = _hist_pad(nh)

    @pl.when(c == 0)
    def _():
        u_ref[...] = _rms(x_ref[...], gpre_ref[...]).reshape(nb * tt, d).astype(BF16)

    u = u_ref[...]
    gb = _dot(u, wb_ref[...])
    v = (_dot(u, wc_ref[...]) * _dot(u, wh_ref[...])).reshape(nb, tt, tn)

    @pl.when(t == 0)
    def _():
        ext_ref[:, hp - nh:hp, :] = hist_ref[...]

    @pl.when(t > 0)
    def _():
        ext_ref[:, hp - nh:hp, :] = carry_ref[c, :, 0:nh, :]

    ext_ref[:, hp:hp + tt, :] = v
    last = ext_ref[:, hp + tt - nh:hp + tt, :]
    carry_ref[c, :, 0:nh, :] = last
    for k in range(d // tn):
        @pl.when(c == k)
        def _(k=k):
            nh_ref[:, :, k * tn:(k + 1) * tn] = last

    wconv = wconv_ref[...]
    conv = wconv[width - 1:width, :] * v
    for k in range(width - 1):
        conv = conv + wconv[k:k + 1, :] * ext_ref[:, hp - nh + k:hp - nh + k + tt, :]
    y = (gb.reshape(nb, tt, tn) * conv).reshape(nb * tt, tn).astype(BF16)

    @pl.when(c == 0)
    def _():
        o_ref[...] = _dot(y, wout_ref[...]).reshape(nb, tt, d)

    @pl.when(c > 0)
    def _():
        o_ref[...] += _dot(y, wout_ref[...]).reshape(nb, tt, d)

    @pl.when(c == nc - 1)
    def _():
        o_ref[...] = x_ref[...] + _rms(o_ref[...], gpost_ref[...])


def _sconv_mixer(x, hist, g_pre, g_post, w_in, w_conv, w_out, slot, gidx):
    b, t, d = x.shape
    nb, tt = _seq_tiles(b, t)
    width = w_conv.shape[1]
    nh = width - 1
    assert tt >= nh
    tn = min(COL_CHUNK, d)
    nc = d // tn
    hp = _hist_pad(nh)
    blocks = _nbytes((nb, tt, d), F32) * 2 + _nbytes((d, tn), BF16) * 4
    scratch = (_nbytes((nb * tt, d), BF16) + _nbytes((nc, nb, SUBLANES, tn), F32)
               + _nbytes((nb, hp + tt, tn), F32))
    temps = _nbytes((nb * tt, tn), F32) * 6 + _nbytes((nb * tt, d), F32)
    return pl.pallas_call(
        _sconv_kernel,
        grid=(b // nb, t // tt, nc),
        in_specs=[
            pl.BlockSpec((nb, tt, d), lambda i, j, c: (i, j, 0)),
            pl.BlockSpec((nb, nh, tn), lambda i, j, c: (i, 0, c)),
            pl.BlockSpec((None, 1, d), lambda i, j, c: (gidx, 0, 0)),
            pl.BlockSpec((None, 1, d), lambda i, j, c: (gidx, 0, 0)),
            pl.BlockSpec((None, d, tn), lambda i, j, c: (slot, 0, c)),
            pl.BlockSpec((None, d, tn), lambda i, j, c: (slot, 0, nc + c)),
            pl.BlockSpec((None, d, tn), lambda i, j, c: (slot, 0, 2 * nc + c)),
            pl.BlockSpec((None, width, tn), lambda i, j, c: (slot, 0, c)),
            pl.BlockSpec((None, tn, d), lambda i, j, c: (slot, c, 0)),
        ],
        out_specs=[
            pl.BlockSpec((nb, tt, d), lambda i, j, c: (i, j, 0)),
            pl.BlockSpec((nb, nh, d), lambda i, j, c: (i, 0, 0)),
        ],
        out_shape=[jax.ShapeDtypeStruct((b, t, d), F32),
                   jax.ShapeDtypeStruct((b, nh, d), F32)],
        scratch_shapes=[pltpu.VMEM((nb * tt, d), BF16),
                        pltpu.VMEM((nc, nb, SUBLANES, tn), F32),
                        pltpu.VMEM((nb, hp + tt, tn), F32)],
        compiler_params=pltpu.CompilerParams(
            dimension_semantics=("parallel", "arbitrary", "arbitrary"),
            vmem_limit_bytes=_vmem_limit(blocks, scratch, temps)),
        name="sconv_mixer",
    )(x, hist, g_pre, g_post, w_in, w_in, w_in, w_conv, w_out)


def _cconv_kernel(x_ref, hist_ref, gpre_ref, gpost_ref, wa_ref, wg_ref, ba_ref, bg_ref,
                  wdw_ref, bdw_ref, lng_ref, lnb_ref, w2_ref, b2_ref,
                  o_ref, nh_ref, u_ref, hc_ref, carry_ref, ext_ref):
    t = pl.program_id(1)
    c = pl.program_id(2)
    nc = pl.num_programs(2) // 2
    nb, tt, d = x_ref.shape
    tn = wa_ref.shape[-1]
    width = wdw_ref.shape[0]
    nh = width - 1
    hp = _hist_pad(nh)
    m = nb * tt

    @pl.when(c == 0)
    def _():
        u_ref[...] = _rms(x_ref[...], gpre_ref[...]).reshape(m, d).astype(BF16)

    @pl.when(c < nc)
    def _():
        u = u_ref[...]
        a = _dot(u, wa_ref[...]) + ba_ref[...]
        gate = _dot(u, wg_ref[...]) + bg_ref[...]
        v = (a * jax.nn.sigmoid(gate)).reshape(nb, tt, tn)

        @pl.when(t == 0)
        def _():
            ext_ref[:, hp - nh:hp, :] = hist_ref[...]

        @pl.when(t > 0)
        def _():
            ext_ref[:, hp - nh:hp, :] = carry_ref[c, :, 0:nh, :]

        ext_ref[:, hp:hp + tt, :] = v
        last = ext_ref[:, hp + tt - nh:hp + tt, :]
        carry_ref[c, :, 0:nh, :] = last
        for k in range(d // tn):
            @pl.when(c == k)
            def _(k=k):
                nh_ref[:, :, k * tn:(k + 1) * tn] = last

        wdw = wdw_ref[...]
        conv = wdw[width - 1:width, :] * v + bdw_ref[...]
        for k in range(width - 1):
            conv = conv + wdw[k:k + 1, :] * ext_ref[:, hp - nh + k:hp - nh + k + tt, :]
        hc_ref[c] = conv.reshape(m, tn)

    @pl.when(c == nc)
    def _():
        s1 = jnp.zeros((m, 1), F32)
        for k in range(nc):
            s1 = s1 + jnp.sum(hc_ref[k], axis=-1, keepdims=True)
        mean = s1 / d
        s2 = jnp.zeros((m, 1), F32)
        for k in range(nc):
            xc = hc_ref[k] - mean
            s2 = s2 + jnp.sum(xc * xc, axis=-1, keepdims=True)
        rstd = lax.rsqrt(s2 / d + EPS)
        for k in range(nc):
            sl = slice(k * tn, (k + 1) * tn)
            hn = (hc_ref[k] - mean) * rstd * lng_ref[:, sl] + lnb_ref[:, sl]
            u_ref[:, sl] = (hn * jax.nn.sigmoid(hn)).astype(BF16)

    for k in range(nc):
        @pl.when(c == nc + k)
        def _(k=k):
            sl = slice(k * tn, (k + 1) * tn)
            y = _dot(u_ref[...], w2_ref[...]) + b2_ref[:, sl]
            o_ref[:, :, sl] = y.reshape(nb, tt, tn)

    @pl.when(c == 2 * nc - 1)
    def _():
        o_ref[...] = x_ref[...] + _rms(o_ref[...], gpost_ref[...])


def _cconv_mixer(x, hist, g_pre, g_post, w_pw1, b_pw1, w_dw, b_dw, ln_g, ln_b,
                 w_pw2, b_pw2, slot, gidx):
    b, t, d = x.shape
    nb, tt = _seq_tiles(b, t)
    width = w_dw.shape[1]
    nh = width - 1
    assert tt >= nh
    tn = min(COL_CHUNK, d)
    nc = d // tn
    hp = _hist_pad(nh)
    m = nb * tt

    def p1(c):
        return jnp.minimum(c, nc - 1)

    def p2(c):
        return jnp.maximum(c - nc, 0)

    blocks = (_nbytes((nb, tt, d), F32) * 2 + _nbytes((d, tn), BF16) * 3
              + _nbytes((nb, nh, tn), F32) * 2)
    scratch = (_nbytes((m, d), BF16) + _nbytes((nc, m, tn), F32)
               + _nbytes((nc, nb, hp, tn), F32) + _nbytes((nb, hp + tt, tn), F32))
    temps = _nbytes((m, tn), F32) * 6 + _nbytes((m, d), F32)
    return pl.pallas_call(
        _cconv_kernel,
        grid=(b // nb, t // tt, 2 * nc),
        in_specs=[
            pl.BlockSpec((nb, tt, d), lambda i, j, c: (i, j, 0)),
            pl.BlockSpec((nb, nh, tn), lambda i, j, c: (i, 0, p1(c))),
            pl.BlockSpec((None, 1, d), lambda i, j, c: (gidx, 0, 0)),
            pl.BlockSpec((None, 1, d), lambda i, j, c: (gidx, 0, 0)),
            pl.BlockSpec((None, d, tn), lambda i, j, c: (slot, 0, p1(c))),
            pl.BlockSpec((None, d, tn), lambda i, j, c: (slot, 0, nc + p1(c))),
            pl.BlockSpec((None, 1, tn), lambda i, j, c: (slot, 0, p1(c))),
            pl.BlockSpec((None, 1, tn), lambda i, j, c: (slot, 0, nc + p1(c))),
            pl.BlockSpec((None, width, tn), lambda i, j, c: (slot, 0, p1(c))),
            pl.BlockSpec((None, 1, tn), lambda i, j, c: (slot, 0, p1(c))),
            pl.BlockSpec((None, 1, d), lambda i, j, c: (slot, 0, 0)),
            pl.BlockSpec((None, 1, d), lambda i, j, c: (slot, 0, 0)),
            pl.BlockSpec((None, d, tn), lambda i, j, c: (slot, 0, p2(c))),
            pl.BlockSpec((None, 1, d), lambda i, j, c: (slot, 0, 0)),
        ],
        out_specs=[
            pl.BlockSpec((nb, tt, d), lambda i, j, c: (i, j, 0)),
            pl.BlockSpec((nb, nh, d), lambda i, j, c: (i, 0, 0)),
        ],
        out_shape=[jax.ShapeDtypeStruct((b, t, d), F32),
                   jax.ShapeDtypeStruct((b, nh, d), F32)],
        scratch_shapes=[pltpu.VMEM((m, d), BF16),
                        pltpu.VMEM((nc, m, tn), F32),
                        pltpu.VMEM((nc, nb, hp, tn), F32),
                        pltpu.VMEM((nb, hp + tt, tn), F32)],
        compiler_params=pltpu.CompilerParams(
            dimension_semantics=("parallel", "arbitrary", "arbitrary"),
            vmem_limit_bytes=_vmem_limit(blocks, scratch, temps)),
        name="cconv_mixer",
    )(x, hist, g_pre, g_post, w_pw1, w_pw1, b_pw1, b_pw1, w_dw, b_dw, ln_g, ln_b,
      w_pw2, b_pw2)


def _xattn_kernel(x_ref, k_ref, v_ref, gpre_ref, gpost_ref, wq_ref, wo_ref,
                  o_ref, u_ref):
    h = pl.program_id(2)
    nheads = pl.num_programs(2)
    nb, tt, d = x_ref.shape
    hd = wq_ref.shape[-1]
    m = nb * tt

    @pl.when(h == 0)
    def _():
        u_ref[...] = _rms(x_ref[...], gpre_ref[...]).reshape(m, d).astype(BF16)

    q = _dot(u_ref[...], wq_ref[...]).astype(BF16)
    outs = []
    for b in range(nb):
        qb = q[b * tt:(b + 1) * tt, :]
        kb = k_ref[b].astype(BF16)
        vb = v_ref[b].astype(BF16)
        s = lax.dot_general(qb, kb, (((1,), (1,)), ((), ())),
                            preferred_element_type=F32) * (hd ** -0.5)
        e = jnp.exp(s - jnp.max(s, axis=-1, keepdims=True))
        p = (e / jnp.sum(e, axis=-1, keepdims=True)).astype(BF16)
        outs.append(_dot(p, vb).astype(BF16))
    o = outs[0] if nb == 1 else jnp.concatenate(outs, axis=0)

    @pl.when(h == 0)
    def _():
        o_ref[...] = _dot(o, wo_ref[...]).reshape(nb, tt, d)

    @pl.when(h > 0)
    def _():
        o_ref[...] += _dot(o, wo_ref[...]).reshape(nb, tt, d)

    @pl.when(h == nheads - 1)
    def _():
        o_ref[...] = x_ref[...] + _rms(o_ref[...], gpost_ref[...])


def _xattn(x, mem_k, mem_v, g_pre, g_post, w_q, w_o, nheads, layer, gidx):
    b, t, d = x.shape
    nb, tt = _seq_tiles(b, t, ROW_TILE if t >= ROW_TILE else ROW_TILE // 2)
    n_mem = mem_k.shape[2]
    hd = d // nheads
    m = nb * tt
    blocks = (_nbytes((nb, tt, d), F32) * 2 + _nbytes((nb, n_mem, hd), F32) * 2
              + _nbytes((d, hd), BF16) * 2)
    scratch = _nbytes((m, d), BF16)
    temps = _nbytes((m, hd), F32) * 4 + _nbytes((m, d), F32) + _nbytes((m, n_mem), F32) * 4
    return pl.pallas_call(
        _xattn_kernel,
        grid=(b // nb, t // tt, nheads),
        in_specs=[
            pl.BlockSpec((nb, tt, d), lambda i, j, h: (i, j, 0)),
            pl.BlockSpec((None, nb, n_mem, hd), lambda i, j, h: (layer, i, 0, h)),
            pl.BlockSpec((None, nb, n_mem, hd), lambda i, j, h: (layer, i, 0, h)),
            pl.BlockSpec((None, 1, d), lambda i, j, h: (gidx, 0, 0)),
            pl.BlockSpec((None, 1, d), lambda i, j, h: (gidx, 0, 0)),
            pl.BlockSpec((None, d, hd), lambda i, j, h: (layer, 0, h)),
            pl.BlockSpec((None, hd, d), lambda i, j, h: (layer, h, 0)),
        ],
        out_specs=pl.BlockSpec((nb, tt, d), lambda i, j, h: (i, j, 0)),
        out_shape=jax.ShapeDtypeStruct((b, t, d), F32),
        scratch_shapes=[pltpu.VMEM((m, d), BF16)],
        compiler_params=pltpu.CompilerParams(
            dimension_semantics=("parallel", "arbitrary", "arbitrary"),
            vmem_limit_bytes=_vmem_limit(blocks, scratch, temps)),
        name="xattn",
    )(x, mem_k, mem_v, g_pre, g_post, w_q, w_o)


def _memkv_kernel(mem_ref, g_ref, wk_ref, wv_ref, k_ref, v_ref, m_ref):
    c = pl.program_id(2)

    @pl.when(c == 0)
    def _():
        m_ref[...] = _rms(mem_ref[...], g_ref[...]).astype(BF16)

    mm = m_ref[...]
    k_ref[...] = _dot(mm, wk_ref[...])
    v_ref[...] = _dot(mm, wv_ref[...])


def _mem_kv(mem, g_mem, w_k, w_v):
    n, d = mem.shape
    depth = w_k.shape[0]
    tm = min(ROW_TILE, n)
    tn = min(COL_CHUNK, d)
    assert n % tm == 0 and d % tn == 0
    blocks = _nbytes((tm, d), F32) + _nbytes((d, tn), BF16) * 2 + _nbytes((tm, tn), F32) * 2
    scratch = _nbytes((tm, d), BF16)
    temps = _nbytes((tm, d), F32) * 2
    out = jax.ShapeDtypeStruct((depth, n, d), F32)
    return pl.pallas_call(
        _memkv_kernel,
        grid=(depth, n // tm, d // tn),
        in_specs=[
            pl.BlockSpec((tm, d), lambda l, i, c: (i, 0)),
            pl.BlockSpec((None, 1, d), lambda l, i, c: (l, 0, 0)),
            pl.BlockSpec((None, d, tn), lambda l, i, c: (l, 0, c)),
            pl.BlockSpec((None, d, tn), lambda l, i, c: (l, 0, c)),
        ],
        out_specs=[pl.BlockSpec((None, tm, tn), lambda l, i, c: (l, i, c)),
                   pl.BlockSpec((None, tm, tn), lambda l, i, c: (l, i, c))],
        out_shape=[out, out],
        scratch_shapes=[pltpu.VMEM((tm, d), BF16)],
        compiler_params=pltpu.CompilerParams(
            dimension_semantics=("parallel", "parallel", "arbitrary"),
            vmem_limit_bytes=_vmem_limit(blocks, scratch, temps)),
        name="mem_kv",
    )(mem, g_mem, w_k, w_v)


def kernel(x_prompt, x_sample, mem_prompt, cache_pool, cache_sconv, cache_cconv, cache_mem_k, cache_mem_v, g_pre, g_post, g_mem, ffn_w_gu, ffn_w_down, xa_w_q, xa_w_k, xa_w_v, xa_w_o, pool_w, pool_scale, sc_w_in, sc_w_conv, sc_w_out, cc_w_pw1, cc_b_pw1, cc_w_dw, cc_b_dw, cc_ln_g, cc_ln_b, cc_w_pw2, cc_b_pw2):
    depth = g_pre.shape[0]
    d = x_prompt.shape[-1]
    bp = x_prompt.shape[0]
    n_mem = mem_prompt.shape[1]
    heads, head_dim = cache_mem_k.shape[3], cache_mem_k.shape[4]

    w_gu, w_down = ffn_w_gu.astype(BF16), ffn_w_down.astype(BF16)
    w_q, w_k, w_v, w_o = (w.astype(BF16) for w in (xa_w_q, xa_w_k, xa_w_v, xa_w_o))
    w_pool = pool_w.astype(BF16)
    w_sc_in, w_sc_out = sc_w_in.astype(BF16), sc_w_out.astype(BF16)
    w_pw1, w_pw2 = cc_w_pw1.astype(BF16), cc_w_pw2.astype(BF16)

    def rows(a):
        return a.reshape(-1, 1, a.shape[-1])

    gp, gq = rows(g_pre), rows(g_post)
    gm = rows(g_mem)
    p_scale = rows(pool_scale)
    b_pw1, b_dw, ln_g, ln_b, b_pw2 = (rows(a) for a in (cc_b_pw1, cc_b_dw, cc_ln_g, cc_ln_b, cc_b_pw2))

    mk, mv = _mem_kv(mem_prompt.reshape(bp * n_mem, d), gm, w_k, w_v)
    mem_k_p = mk.reshape(depth, bp, n_mem, d)
    mem_v_p = mv.reshape(depth, bp, n_mem, d)
    mem_k_s = cache_mem_k.reshape(depth, -1, n_mem, d)
    mem_v_s = cache_mem_v.reshape(depth, -1, n_mem, d)

    def trunk(x, pos0, pool_hist, sconv_hist, cconv_hist, mem_k, mem_v):
        b, t, _ = x.shape
        new_pool, new_sconv, new_cconv = [], [], []
        for i in range(depth):
            x = _ffn(x.reshape(b * t, d), gp, gq, w_gu, w_down, i, 0, 4 * i).reshape(b, t, d)
            kind, slot = i % N_MIXERS, i // N_MIXERS
            if kind == 0:
                x, st = _pool_mixer(x, pool_hist[slot], gp, gq, w_pool, p_scale, slot, 4 * i + 1, pos0)
                new_pool.append(st)
            elif kind == 1:
                x, st = _sconv_mixer(x, sconv_hist[slot], gp, gq, w_sc_in, sc_w_conv, w_sc_out,
                                     slot, 4 * i + 1)
                new_sconv.append(st)
            else:
                x, st = _cconv_mixer(x, cconv_hist[slot], gp, gq, w_pw1, b_pw1, cc_w_dw, b_dw,
                                     ln_g, ln_b, w_pw2, b_pw2, slot, 4 * i + 1)
                new_cconv.append(st)
            x = _xattn(x, mem_k, mem_v, gp, gq, w_q, w_o, heads, i, 4 * i + 2)
            x = _ffn(x.reshape(b * t, d), gp, gq, w_gu, w_down, i, 1, 4 * i + 3).reshape(b, t, d)
        return x, jnp.stack(new_pool), jnp.stack(new_sconv), jnp.stack(new_cconv)

    dt = x_prompt.dtype
    n_pool, n_sconv, n_cconv = cache_pool.shape[0], cache_sconv.shape[0], cache_cconv.shape[0]
    pool_h0 = [jnp.zeros((bp, POOL_HIST, d), dt)] * n_pool
    sconv_h0 = [jnp.zeros((bp, cache_sconv.shape[2], d), dt)] * n_sconv
    cconv_h0 = [jnp.zeros((bp, cache_cconv.shape[2], d), dt)] * n_cconv
    y_prompt, pool_p, sconv_p, cconv_p = trunk(x_prompt, 0, pool_h0, sconv_h0, cconv_h0, mem_k_p, mem_v_p)
    y_sample, pool_s, sconv_s, cconv_s = trunk(
        x_sample, PAST_LEN,
        [cache_pool[j] for j in range(n_pool)],
        [cache_sconv[j] for j in range(n_sconv)],
        [cache_cconv[j] for j in range(n_cconv)],
        mem_k_s, mem_v_s)

    mem_k_prompt = mem_k_p.reshape(depth, bp, n_mem, heads, head_dim)
    mem_v_prompt = mem_v_p.reshape(depth, bp, n_mem, heads, head_dim)
    return (y_prompt, y_sample, pool_p, pool_s, sconv_p, sconv_s, cconv_p, cconv_s,
            mem_k_prompt, mem_v_prompt)
```

```python
import functools

import jax
import jax.numpy as jnp
from jax import lax
from jax.experimental import pallas as pl
from jax.experimental.pallas import tpu as pltpu

EPS = 1e-6
MACARON = 0.5
PAST_LEN = 1024
POOL_WINDOWS = (2, 4, 8, 16)
POOL_HIST = max(POOL_WINDOWS) - 1
N_MIXERS = 3

BF16 = jnp.bfloat16
F32 = jnp.float32

V7X_VMEM_BYTES = 64 * 1024 * 1024
SUBLANES = 8

ROW_TILE = 512
FFN_ROW_TILE = 1024
FFN_SUB_ROWS = 512
XATTN_ROW_TILE = 1024
XATTN_SUB_ROWS = 512
XATTN_SHORT_ROWS = 256
COL_CHUNK = 512


def _vmem_limit(block_bytes, scratch_bytes, temp_bytes):
    est = 2 * block_bytes + scratch_bytes + temp_bytes
    return int(min(est + est // 8, V7X_VMEM_BYTES - 6 * 1024 * 1024))


def _nbytes(shape, dtype):
    n = 1
    for s in shape:
        n *= s
    return n * jnp.dtype(dtype).itemsize


def _rms(x, g):
    ms = jnp.mean(x * x, axis=-1, keepdims=True)
    return x * lax.rsqrt(ms + EPS) * g


def _dot(a, b):
    return jnp.dot(a, b, preferred_element_type=F32)


def _hist_pad(h):
    return -(-h // SUBLANES) * SUBLANES


def _post_residual(o_ref, x_ref, g_ref, scale, idx=Ellipsis):
    o = o_ref[idx]
    ms = jnp.mean(o * o, axis=-1, keepdims=True)
    o_ref[idx] = x_ref[idx] + o_ref[idx] * (scale * lax.rsqrt(ms + EPS)) * g_ref[...]


def _causal_depthwise(ext_ref, w, tt, start0):
    width = w.shape[0]
    out = None
    for s in range(SUBLANES):
        taps = [k for k in range(width) if (start0 + k) % SUBLANES == s]
        if not taps:
            continue
        rows = tt if s == 0 else tt + SUBLANES
        part = None
        for k in taps:
            off = start0 + k - s
            term = w[k:k + 1, :] * ext_ref[:, off:off + rows, :]
            part = term if part is None else part + term
        if s:
            part = part[:, s:s + tt, :]
        out = part if out is None else out + part
    return out


def _chunk_variants(c, nc, body):
    if nc == 1:
        body(True, True)
        return
    pl.when(c == 0)(lambda: body(True, False))
    if nc > 2:
        pl.when(jnp.logical_and(c > 0, c < nc - 1))(lambda: body(False, False))
    pl.when(c == nc - 1)(lambda: body(False, True))


def _ffn_kernel(x_ref, gpre_ref, gpost_ref, wg_ref, wu_ref, wd_ref, o_ref, h_ref, *, nj):
    j = pl.program_id(1)

    tm, d = x_ref.shape
    sub = min(tm, FFN_SUB_ROWS)

    def body(first, last):
        for r in range(tm // sub):
            rows = slice(r * sub, (r + 1) * sub)
            if first:
                h_ref[rows, :] = _rms(x_ref[rows, :], gpre_ref[...]).astype(BF16)
            h = h_ref[rows, :]
            gate = _dot(h, wg_ref[...])
            up = _dot(h, wu_ref[...])
            a = (gate * jax.nn.sigmoid(gate) * up).astype(BF16)
            if first:
                o_ref[rows, :] = _dot(a, wd_ref[...])
            else:
                o_ref[rows, :] += _dot(a, wd_ref[...])
            if last:
                _post_residual(o_ref, x_ref, gpost_ref, MACARON, (rows, slice(None)))

    _chunk_variants(j, nj, body)


def _ffn(x, g_pre, g_post, w_gu, w_down, layer, slot, gidx):
    n, d = x.shape
    dff = w_down.shape[2]
    tm = min(FFN_ROW_TILE, n)
    tf = min(COL_CHUNK, dff)
    nj = dff // tf
    assert n % tm == 0 and dff % tf == 0
    blocks = (_nbytes((tm, d), F32) * 2 + _nbytes((d, tf), BF16) * 3)
    scratch = _nbytes((tm, d), BF16)
    temps = _nbytes((min(tm, FFN_SUB_ROWS), tf), F32) * 6
    return pl.pallas_call(
        functools.partial(_ffn_kernel, nj=nj),
        grid=(n // tm, nj),
        in_specs=[
            pl.BlockSpec((tm, d), lambda i, j: (i, 0)),
            pl.BlockSpec((None, 1, d), lambda i, j: (gidx, 0, 0)),
            pl.BlockSpec((None, 1, d), lambda i, j: (gidx, 0, 0)),
            pl.BlockSpec((None, None, d, tf), lambda i, j: (layer, slot, 0, j)),
            pl.BlockSpec((None, None, d, tf), lambda i, j: (layer, slot, 0, nj + j)),
            pl.BlockSpec((None, None, tf, d), lambda i, j: (layer, slot, j, 0)),
        ],
        out_specs=pl.BlockSpec((tm, d), lambda i, j: (i, 0)),
        out_shape=jax.ShapeDtypeStruct((n, d), F32),
        scratch_shapes=[pltpu.VMEM((tm, d), BF16)],
        compiler_params=pltpu.CompilerParams(
            dimension_semantics=("parallel", "arbitrary"),
            vmem_limit_bytes=_vmem_limit(blocks, scratch, temps)),
        name="ffn",
    )(x, g_pre, g_post, w_gu, w_gu, w_down)


def _seq_tiles(b, t, rows=None):
    rows = ROW_TILE if rows is None else rows
    if t >= rows:
        assert t % rows == 0
        return 1, rows
    nb = min(b, max(1, rows // t))
    assert b % nb == 0 and t % SUBLANES == 0
    return nb, t


def _pool_kernel(x_ref, hist_ref, gpre_ref, gpost_ref, w_ref, scale_ref,
                 o_ref, nh_ref, ext_ref, *, pos0):
    t = pl.program_id(1)
    nt = pl.num_programs(1)
    nb, tt, d = x_ref.shape
    hp = _hist_pad(POOL_HIST)
    gc = d // len(POOL_WINDOWS)

    @pl.when(t == 0)
    def _():
        ext_ref[:, hp - POOL_HIST:hp, :] = hist_ref[...]

    @pl.when(t > 0)
    def _():
        ext_ref[:, hp - POOL_HIST:hp, :] = ext_ref[:, hp + tt - POOL_HIST:hp + tt, :]

    x = x_ref[...]
    ext_ref[:, hp:hp + tt, :] = _rms(x, gpre_ref[...])
    pos = pos0 + t * tt + lax.broadcasted_iota(jnp.int32, (1, tt, 1), 1)
    ys = []
    for g, w in enumerate(POOL_WINDOWS):
        sl = slice(g * gc, (g + 1) * gc)
        u = ext_ref[:, hp:hp + tt, sl]
        win = u
        for k in range(1, w):
            win = win + ext_ref[:, hp - k:hp - k + tt, sl]
        cnt = jnp.minimum(pos + 1, w).astype(F32)
        dg = (win / cnt - u).reshape(nb * tt, gc).astype(BF16)
        ys.append(_dot(dg, w_ref[g]))
    y = jnp.concatenate(ys, axis=-1) * scale_ref[...]
    out = x.reshape(nb * tt, d) + _rms(y, gpost_ref[...])
    o_ref[...] = out.reshape(nb, tt, d)

    @pl.when(t == nt - 1)
    def _():
        nh_ref[...] = ext_ref[:, hp + tt - POOL_HIST:hp + tt, :]


def _pool_mixer(x, hist, g_pre, g_post, pool_w, pool_scale, slot, gidx, pos0):
    b, t, d = x.shape
    nb, tt = _seq_tiles(b, t)
    assert tt >= POOL_HIST
    ng = len(POOL_WINDOWS)
    gc = d // ng
    hp = _hist_pad(POOL_HIST)
    blocks = _nbytes((nb, tt, d), F32) * 2 + _nbytes((ng, gc, gc), BF16)
    scratch = _nbytes((nb, hp + tt, d), F32)
    temps = _nbytes((nb, tt, d), F32) * 4
    return pl.pallas_call(
        functools.partial(_pool_kernel, pos0=pos0),
        grid=(b // nb, t // tt),
        in_specs=[
            pl.BlockSpec((nb, tt, d), lambda i, j: (i, j, 0)),
            pl.BlockSpec((nb, POOL_HIST, d), lambda i, j: (i, 0, 0)),
            pl.BlockSpec((None, 1, d), lambda i, j: (gidx, 0, 0)),
            pl.BlockSpec((None, 1, d), lambda i, j: (gidx, 0, 0)),
            pl.BlockSpec((None, ng, gc, gc), lambda i, j: (slot, 0, 0, 0)),
            pl.BlockSpec((None, 1, d), lambda i, j: (slot, 0, 0)),
        ],
        out_specs=[
            pl.BlockSpec((nb, tt, d), lambda i, j: (i, j, 0)),
            pl.BlockSpec((nb, POOL_HIST, d), lambda i, j: (i, 0, 0)),
        ],
        out_shape=[jax.ShapeDtypeStruct((b, t, d), F32),
                   jax.ShapeDtypeStruct((b, POOL_HIST, d), F32)],
        scratch_shapes=[pltpu.VMEM((nb, hp + tt, d), F32)],
        compiler_params=pltpu.CompilerParams(
            dimension_semantics=("parallel", "arbitrary"),
            vmem_limit_bytes=_vmem_limit(blocks, scratch, temps)),
        name="pool_mixer",
    )(x, hist, g_pre, g_post, pool_w, pool_scale)


def _sconv_kernel(x_ref, hist_ref, gpre_ref, gpost_ref, wb_ref, wc_ref, wh_ref,
                  wconv_ref, wout_ref, o_ref, nh_ref, u_ref, carry_ref, ext_ref, *, nc):
    t = pl.program_id(1)
    nt = pl.num_programs(1)
    c = pl.program_id(2)
    nb, tt, d = x_ref.shape
    tn = wb_ref.shape[-1]
    width = wconv_ref.shape[0]
    nh = width - 1
    hp = _hist_pad(nh)

    @pl.when(t == 0)
    def _():
        ext_ref[:, 0:hp, :] = jnp.zeros((nb, hp, tn), F32)
        ext_ref[:, hp - nh:hp, :] = hist_ref[...]

    @pl.when(t > 0)
    def _():
        ext_ref[:, hp - nh:hp, :] = carry_ref[c, :, 0:nh, :]

    def body(first, last):
        if first:
            u_ref[...] = _rms(x_ref[...], gpre_ref[...]).reshape(nb * tt, d).astype(BF16)
        u = u_ref[...]
        gb = _dot(u, wb_ref[...])
        v = (_dot(u, wc_ref[...]) * _dot(u, wh_ref[...])).reshape(nb, tt, tn)
        ext_ref[:, hp:hp + tt, :] = v
        carry_ref[c, :, 0:nh, :] = v[:, tt - nh:tt, :]
        conv = _causal_depthwise(ext_ref, wconv_ref[...], tt, hp - nh)
        y =(gb.reshape(nb, tt, tn) * conv).reshape(nb * tt, tn).astype(BF16)
        if first:
            o_ref[...] = _dot(y, wout_ref[...]).reshape(nb, tt, d)
        else:
            o_ref[...] += _dot(y, wout_ref[...]).reshape(nb, tt, d)
        if last:
            _post_residual(o_ref, x_ref, gpost_ref, 1.0)

    _chunk_variants(c, nc, body)

    @pl.when(jnp.logical_and(t == nt - 1, c == nc - 1))
    def _():
        for k in range(nc):
            nh_ref[:, :, k * tn:(k + 1) * tn] = carry_ref[k, :, 0:nh, :]


def _sconv_mixer(x, hist, g_pre, g_post, w_in, w_conv, w_out, slot, gidx):
    b, t, d = x.shape
    nb, tt = _seq_tiles(b, t)
    width = w_conv.shape[1]
    nh = width - 1
    assert tt >= nh
    tn = min(COL_CHUNK, d)
    nc = d // tn
    hp = _hist_pad(nh)
    blocks = _nbytes((nb, tt, d), F32) * 2 + _nbytes((d, tn), BF16) * 4
    scratch = (_nbytes((nb * tt, d), BF16) + _nbytes((nc, nb, SUBLANES, tn), F32)
               + _nbytes((nb, hp + tt, tn), F32))
    temps = _nbytes((nb * tt, tn), F32) * 6 + _nbytes((nb * tt, d), F32)
    return pl.pallas_call(
        functools.partial(_sconv_kernel, nc=nc),
        grid=(b // nb, t // tt, nc),
        in_specs=[
            pl.BlockSpec((nb, tt, d), lambda i, j, c: (i, j, 0)),
            pl.BlockSpec((nb, nh, tn), lambda i, j, c: (i, 0, c)),
            pl.BlockSpec((None, 1, d), lambda i, j, c: (gidx, 0, 0)),
            pl.BlockSpec((None, 1, d), lambda i, j, c: (gidx, 0, 0)),
            pl.BlockSpec((None, d, tn), lambda i, j, c: (slot, 0, c)),
            pl.BlockSpec((None, d, tn), lambda i, j, c: (slot, 0, nc + c)),
            pl.BlockSpec((None, d, tn), lambda i, j, c: (slot, 0, 2 * nc + c)),
            pl.BlockSpec((None, width, tn), lambda i, j, c: (slot, 0, c)),
            pl.BlockSpec((None, tn, d), lambda i, j, c: (slot, c, 0)),
        ],
        out_specs=[
            pl.BlockSpec((nb, tt, d), lambda i, j, c: (i, j, 0)),
            pl.BlockSpec((nb, nh, d), lambda i, j, c: (i, 0, 0)),
        ],
        out_shape=[jax.ShapeDtypeStruct((b, t, d), F32),
                   jax.ShapeDtypeStruct((b, nh, d), F32)],
        scratch_shapes=[pltpu.VMEM((nb * tt, d), BF16),
                        pltpu.VMEM((nc, nb, SUBLANES, tn), F32),
                        pltpu.VMEM((nb, hp + tt, tn), F32)],
        compiler_params=pltpu.CompilerParams(
            dimension_semantics=("parallel", "arbitrary", "arbitrary"),
            vmem_limit_bytes=_vmem_limit(blocks, scratch, temps)),
        name="sconv_mixer",
    )(x, hist, g_pre, g_post, w_in, w_in, w_in, w_conv, w_out)


def _cconv_kernel(x_ref, hist_ref, gpre_ref, gpost_ref, wa_ref, wg_ref, ba_ref, bg_ref,
                  wdw_ref, bdw_ref, lng_ref, lnb_ref, w2_ref, b2_ref,
                  o_ref, nh_ref, u_ref, hc_ref, carry_ref, ext_ref, *, nc):
    t = pl.program_id(1)
    nt = pl.num_programs(1)
    c = pl.program_id(2)
    nb, tt, d = x_ref.shape
    tn = wa_ref.shape[-1]
    width = wdw_ref.shape[0]
    nh = width - 1
    hp = _hist_pad(nh)
    m = nb * tt

    @pl.when(jnp.logical_and(c < nc, t == 0))
    def _():
        ext_ref[:, 0:hp, :] = jnp.zeros((nb, hp, tn), F32)
        ext_ref[:, hp - nh:hp, :] = hist_ref[...]

    @pl.when(jnp.logical_and(c < nc, t > 0))
    def _():
        ext_ref[:, hp - nh:hp, :] = carry_ref[c, :, 0:nh, :]

    def project_and_conv(first):
        if first:
            u_ref[...] = _rms(x_ref[...], gpre_ref[...]).reshape(m, d).astype(BF16)
        u = u_ref[...]
        a = _dot(u, wa_ref[...]) + ba_ref[...]
        gate = _dot(u, wg_ref[...]) + bg_ref[...]
        v = (a * jax.nn.sigmoid(gate)).reshape(nb, tt, tn)
        ext_ref[:, hp:hp + tt, :] = v
        carry_ref[c, :, 0:nh, :] = v[:, tt - nh:tt, :]
        conv = _causal_depthwise(ext_ref, wdw_ref[...], tt, hp - nh) + bdw_ref[...]
        hc_ref[c] = conv.reshape(m, tn)

    pl.when(c == 0)(lambda: project_and_conv(True))
    if nc > 1:
        pl.when(jnp.logical_and(c > 0, c < nc))(lambda: project_and_conv(False))

    def layernorm_silu():
        s1 = jnp.zeros((m, 1), F32)
        for k in range(nc):
            s1 = s1 + jnp.sum(hc_ref[k], axis=-1, keepdims=True)
        mean = s1 / d
        s2 = jnp.zeros((m, 1), F32)
        for k in range(nc):
            xc = hc_ref[k] - mean
            s2 = s2 + jnp.sum(xc * xc, axis=-1, keepdims=True)
        rstd = lax.rsqrt(s2 / d + EPS)
        for k in range(nc):
            sl = slice(k * tn, (k + 1) * tn)
            hn = (hc_ref[k] - mean) * rstd * lng_ref[:, sl] + lnb_ref[:, sl]
            u_ref[:, sl] = (hn * jax.nn.sigmoid(hn)).astype(BF16)

    for k in range(nc):
        @pl.when(c == nc + k)
        def _(k=k):
            if k == 0:
                layernorm_silu()
            sl = slice(k * tn, (k + 1) * tn)
            y = _dot(u_ref[...], w2_ref[...]) + b2_ref[:, sl]
            o_ref[:, :, sl] = y.reshape(nb, tt, tn)
            if k == nc - 1:
                _post_residual(o_ref, x_ref, gpost_ref, 1.0)

    @pl.when(jnp.logical_and(t == nt - 1, c == 2 * nc - 1))
    def _():
        for k in range(nc):
            nh_ref[:, :, k * tn:(k + 1) * tn] = carry_ref[k, :, 0:nh, :]


def _cconv_mixer(x, hist, g_pre, g_post, w_pw1, b_pw1, w_dw, b_dw, ln_g, ln_b,
                 w_pw2, b_pw2, slot, gidx):
    b, t, d = x.shape
    nb, tt = _seq_tiles(b, t)
    width = w_dw.shape[1]
    nh = width - 1
    assert tt >= nh
    tn = min(COL_CHUNK, d)
    nc = d // tn
    hp = _hist_pad(nh)
    m = nb * tt

    def p1(c):
        return jnp.minimum(c, nc - 1)

    def p2(c):
        return jnp.maximum(c - nc, 0)

    blocks = (_nbytes((nb, tt, d), F32) * 2 + _nbytes((d, tn), BF16) * 3
              + _nbytes((nb, nh, tn), F32) * 2)
    scratch = (_nbytes((m, d), BF16) + _nbytes((nc, m, tn), F32)
               + _nbytes((nc, nb, hp, tn), F32) + _nbytes((nb, hp + tt, tn), F32))
    temps = _nbytes((m, tn), F32) * 6 + _nbytes((m, d), F32)
    return pl.pallas_call(
        functools.partial(_cconv_kernel, nc=nc),
        grid=(b // nb, t // tt, 2 * nc),
        in_specs=[
            pl.BlockSpec((nb, tt, d), lambda i, j, c: (i, j, 0)),
            pl.BlockSpec((nb, nh, tn), lambda i, j, c: (i, 0, p1(c))),
            pl.BlockSpec((None, 1, d), lambda i, j, c: (gidx, 0, 0)),
            pl.BlockSpec((None, 1, d), lambda i, j, c: (gidx, 0, 0)),
            pl.BlockSpec((None, d, tn), lambda i, j, c: (slot, 0, p1(c))),
            pl.BlockSpec((None, d, tn), lambda i, j, c: (slot, 0, nc + p1(c))),
            pl.BlockSpec((None, 1, tn), lambda i, j, c: (slot, 0, p1(c))),
            pl.BlockSpec((None, 1, tn), lambda i, j, c: (slot, 0, nc + p1(c))),
            pl.BlockSpec((None, width, tn), lambda i, j, c: (slot, 0, p1(c))),
            pl.BlockSpec((None, 1, tn), lambda i, j, c: (slot, 0, p1(c))),
            pl.BlockSpec((None, 1, d), lambda i, j, c: (slot, 0, 0)),
            pl.BlockSpec((None, 1, d), lambda i, j, c: (slot, 0, 0)),
            pl.BlockSpec((None, d, tn), lambda i, j, c: (slot, 0, p2(c))),
            pl.BlockSpec((None, 1, d), lambda i, j, c: (slot, 0, 0)),
        ],
        out_specs=[
            pl.BlockSpec((nb, tt, d), lambda i, j, c: (i, j, 0)),
            pl.BlockSpec((nb, nh, d), lambda i, j, c: (i, 0, 0)),
        ],
        out_shape=[jax.ShapeDtypeStruct((b, t, d), F32),
                   jax.ShapeDtypeStruct((b, nh, d), F32)],
        scratch_shapes=[pltpu.VMEM((m, d), BF16),
                        pltpu.VMEM((nc, m, tn), F32),
                        pltpu.VMEM((nc, nb, hp, tn), F32),
                        pltpu.VMEM((nb, hp + tt, tn), F32)],
        compiler_params=pltpu.CompilerParams(
            dimension_semantics=("parallel", "arbitrary", "arbitrary"),
            vmem_limit_bytes=_vmem_limit(blocks, scratch, temps)),
        name="cconv_mixer",
    )(x, hist, g_pre, g_post, w_pw1, w_pw1, b_pw1, b_pw1, w_dw, b_dw, ln_g, ln_b,
      w_pw2, b_pw2)


def _xattn_kernel(x_ref, k_ref, v_ref, gpre_ref, gpost_ref, wq_ref, wo_ref,
                  o_ref, u_ref, *, nheads, tt):
    h = pl.program_id(2)
    m, d = x_ref.shape
    nb = k_ref.shape[0]
    hd = wq_ref.shape[-1]
    sub = min(m, XATTN_SUB_ROWS)
    seq_rows = min(sub, tt)

    def body(first, last):
        for r in range(m // sub):
            rows = slice(r * sub, (r + 1) * sub)
            if first:
                u_ref[rows, :] = _rms(x_ref[rows, :], gpre_ref[...]).astype(BF16)
            q = _dot(u_ref[rows, :], wq_ref[...]).astype(BF16)
            outs = []
            for s0 in range(0, sub, seq_rows):
                b = (r * sub + s0) // tt if nb > 1 else 0
                kb = k_ref[b].astype(BF16)
                vb = v_ref[b].astype(BF16)
                s = lax.dot_general(q[s0:s0 + seq_rows, :], kb, (((1,), (1,)), ((), ())),
                                    preferred_element_type=F32) * (hd ** -0.5)
                e = jnp.exp(s - jnp.max(s, axis=-1, keepdims=True))
                p = (e / jnp.sum(e, axis=-1, keepdims=True)).astype(BF16)
                outs.append(_dot(p, vb).astype(BF16))
            o = outs[0] if len(outs) == 1 else jnp.concatenate(outs, axis=0)
            if first:
                o_ref[rows, :] = _dot(o, wo_ref[...])
            else:
                o_ref[rows, :] += _dot(o, wo_ref[...])
            if last:
                _post_residual(o_ref, x_ref, gpost_ref, 1.0, (rows, slice(None)))

    _chunk_variants(h, nheads, body)


def _xattn(x, mem_k, mem_v, g_pre, g_post, w_q, w_o, nheads, layer, gidx):
    b, t, d = x.shape
    nb, tt = _seq_tiles(b, t, XATTN_ROW_TILE if t >= XATTN_ROW_TILE else XATTN_SHORT_ROWS)
    n_mem = mem_k.shape[2]
    hd = d // nheads
    m = nb * tt
    nt = t // tt
    sub = min(m, XATTN_SUB_ROWS)
    assert m % sub == 0 and (sub % tt == 0 or tt % sub == 0)
    blocks = (_nbytes((m, d), F32) * 2 + _nbytes((nb, n_mem, hd), F32) * 2
              + _nbytes((d, hd), BF16) * 2)
    scratch = _nbytes((m, d), BF16)
    temps = _nbytes((sub, hd), F32) * 4 + _nbytes((sub, n_mem), F32) * 4
    out = pl.pallas_call(
        functools.partial(_xattn_kernel, nheads=nheads, tt=tt),
        grid=(b // nb, nt, nheads),
        in_specs=[
            pl.BlockSpec((m, d), lambda i, j, h: (i * nt + j, 0)),
            pl.BlockSpec((None, nb, n_mem, hd), lambda i, j, h: (layer, i, 0, h)),
            pl.BlockSpec((None, nb, n_mem, hd), lambda i, j, h: (layer, i, 0, h)),
            pl.BlockSpec((None, 1, d), lambda i, j, h: (gidx, 0, 0)),
            pl.BlockSpec((None, 1, d), lambda i, j, h: (gidx, 0, 0)),
            pl.BlockSpec((None, d, hd), lambda i, j, h: (layer, 0, h)),
            pl.BlockSpec((None, hd, d), lambda i, j, h: (layer, h, 0)),
        ],
        out_specs=pl.BlockSpec((m, d), lambda i, j, h: (i * nt + j, 0)),
        out_shape=jax.ShapeDtypeStruct((b * t, d), F32),
        scratch_shapes=[pltpu.VMEM((m, d), BF16)],
        compiler_params=pltpu.CompilerParams(
            dimension_semantics=("parallel", "arbitrary", "arbitrary"),
            vmem_limit_bytes=_vmem_limit(blocks, scratch, temps)),
        name="xattn",
    )(x.reshape(b * t, d), mem_k, mem_v, g_pre, g_post, w_q, w_o)
    return out.reshape(b, t, d)


def _memkv_kernel(mem_ref, g_ref, wk_ref, wv_ref, k_ref, v_ref, m_ref):
    c = pl.program_id(2)

    @pl.when(c == 0)
    def _():
        m_ref[...] = _rms(mem_ref[...], g_ref[...]).astype(BF16)

    mm = m_ref[...]
    k_ref[...] = _dot(mm, wk_ref[...])
    v_ref[...] = _dot(mm, wv_ref[...])


def _mem_kv(mem, g_mem, w_k, w_v):
    n, d = mem.shape
    depth = w_k.shape[0]
    tm = min(ROW_TILE, n)
    tn = min(COL_CHUNK, d)
    assert n % tm == 0 and d % tn == 0
    blocks = _nbytes((tm, d), F32) + _nbytes((d, tn), BF16) * 2 + _nbytes((tm, tn), F32) * 2
    scratch = _nbytes((tm, d), BF16)
    temps = _nbytes((tm, d), F32) * 2
    out = jax.ShapeDtypeStruct((depth, n, d), F32)
    return pl.pallas_call(
        _memkv_kernel,
        grid=(depth, n // tm, d // tn),
        in_specs=[
            pl.BlockSpec((tm, d), lambda l, i, c: (i, 0)),
            pl.BlockSpec((None, 1, d), lambda l, i, c: (l, 0, 0)),
            pl.BlockSpec((None, d, tn), lambda l, i, c: (l, 0, c)),
            pl.BlockSpec((None, d, tn), lambda l, i, c: (l, 0, c)),
        ],
        out_specs=[pl.BlockSpec((None, tm, tn), lambda l, i, c: (l, i, c)),
                   pl.BlockSpec((None, tm, tn), lambda l, i, c: (l, i, c))],
        out_shape=[out, out],
        scratch_shapes=[pltpu.VMEM((tm, d), BF16)],
        compiler_params=pltpu.CompilerParams(
            dimension_semantics=("parallel", "parallel", "arbitrary"),
            vmem_limit_bytes=_vmem_limit(blocks, scratch, temps)),
        name="mem_kv",
    )(mem, g_mem, w_k, w_v)


def kernel(x_prompt, x_sample, mem_prompt, cache_pool, cache_sconv, cache_cconv, cache_mem_k, cache_mem_v, g_pre, g_post, g_mem, ffn_w_gu, ffn_w_down, xa_w_q, xa_w_k, xa_w_v, xa_w_o, pool_w, pool_scale, sc_w_in, sc_w_conv, sc_w_out, cc_w_pw1, cc_b_pw1, cc_w_dw, cc_b_dw, cc_ln_g, cc_ln_b, cc_w_pw2, cc_b_pw2):
    depth = g_pre.shape[0]
    d = x_prompt.shape[-1]
    bp = x_prompt.shape[0]
    n_mem = mem_prompt.shape[1]
    heads, head_dim = cache_mem_k.shape[3], cache_mem_k.shape[4]

    w_gu, w_down = ffn_w_gu.astype(BF16), ffn_w_down.astype(BF16)
    w_q, w_k, w_v, w_o = (w.astype(BF16) for w in (xa_w_q, xa_w_k, xa_w_v, xa_w_o))
    w_pool = pool_w.astype(BF16)
    w_sc_in, w_sc_out = sc_w_in.astype(BF16), sc_w_out.astype(BF16)
    w_pw1, w_pw2 = cc_w_pw1.astype(BF16), cc_w_pw2.astype(BF16)

    def rows(a):
        return a.reshape(-1, 1, a.shape[-1])

    gp, gq = rows(g_pre), rows(g_post)
    gm = rows(g_mem)
    p_scale = rows(pool_scale)
    b_pw1, b_dw, ln_g, ln_b, b_pw2 = (rows(a) for a in (cc_b_pw1, cc_b_dw, cc_ln_g, cc_ln_b, cc_b_pw2))

    mk, mv = _mem_kv(mem_prompt.reshape(bp * n_mem, d), gm, w_k, w_v)
    mem_k_p = mk.reshape(depth, bp, n_mem, d)
    mem_v_p = mv.reshape(depth, bp, n_mem, d)
    mem_k_s = cache_mem_k.reshape(depth, -1, n_mem, d)
    mem_v_s = cache_mem_v.reshape(depth, -1, n_mem, d)

    def trunk(x, pos0, pool_hist, sconv_hist, cconv_hist, mem_k, mem_v):
        b, t, _ = x.shape
        new_pool, new_sconv, new_cconv = [], [], []
        for i in range(depth):
            x = _ffn(x.reshape(b * t, d), gp, gq, w_gu, w_down, i, 0, 4 * i).reshape(b, t, d)
            kind, slot = i % N_MIXERS, i // N_MIXERS
            if kind == 0:
                x, st = _pool_mixer(x, pool_hist[slot], gp, gq, w_pool, p_scale, slot, 4 * i + 1, pos0)
                new_pool.append(st)
            elif kind == 1:
                x, st = _sconv_mixer(x, sconv_hist[slot], gp, gq, w_sc_in, sc_w_conv, w_sc_out,
                                     slot, 4 * i + 1)
                new_sconv.append(st)
            else:
                x, st = _cconv_mixer(x, cconv_hist[slot], gp, gq, w_pw1, b_pw1, cc_w_dw, b_dw,
                                     ln_g, ln_b, w_pw2, b_pw2, slot, 4 * i + 1)
                new_cconv.append(st)
            x = _xattn(x, mem_k, mem_v, gp, gq, w_q, w_o, heads, i, 4 * i + 2)
            x = _ffn(x.reshape(b * t, d), gp, gq, w_gu, w_down, i, 1, 4 * i + 3).reshape(b, t, d)
        return x, jnp.stack(new_pool), jnp.stack(new_sconv), jnp.stack(new_cconv)

    dt = x_prompt.dtype
    n_pool, n_sconv, n_cconv = cache_pool.shape[0], cache_sconv.shape[0], cache_cconv.shape[0]
    pool_h0 = [jnp.zeros((bp, POOL_HIST, d), dt)] * n_pool
    sconv_h0 = [jnp.zeros((bp, cache_sconv.shape[2], d), dt)] * n_sconv
    cconv_h0 = [jnp.zeros((bp, cache_cconv.shape[2], d), dt)] * n_cconv
    y_prompt, pool_p, sconv_p, cconv_p = trunk(x_prompt, 0, pool_h0, sconv_h0, cconv_h0, mem_k_p, mem_v_p)
    y_sample, pool_s, sconv_s, cconv_s = trunk(
        x_sample, PAST_LEN,
        [cache_pool[j] for j in range(n_pool)],
        [cache_sconv[j] for j in range(n_sconv)],
        [cache_cconv[j] for j in range(n_cconv)],
        mem_k_s, mem_v_s)

    mem_k_prompt = mem_k_p.reshape(depth, bp, n_mem, heads, head_dim)
    mem_v_prompt = mem_v_p.reshape(depth, bp, n_mem, heads, head_dim)
    return (y_prompt, y_sample, pool_p, pool_s, sconv_p, sconv_s, cconv_p, cconv_s,
            mem_k_prompt, mem_v_prompt)
```

```python
import functools

import jax
import jax.numpy as jnp
from jax import lax
from jax.experimental import pallas as pl
from jax.experimental.pallas import tpu as pltpu

EPS = 1e-6
MACARON = 0.5
PAST_LEN = 1024
POOL_WINDOWS = (2, 4, 8, 16)
POOL_HIST = max(POOL_WINDOWS) - 1
N_MIXERS = 3

BF16 = jnp.bfloat16
F32 = jnp.float32

V7X_VMEM_BYTES = 64 * 1024 * 1024
SUBLANES = 8

assert all(w & (w - 1) == 0 for w in POOL_WINDOWS)
POOL_PAD = SUBLANES * (max(POOL_WINDOWS).bit_length() - 1)
assert POOL_PAD >= POOL_HIST + 1

ROW_TILE = 512
FFN_ROW_TILE = 1024
FFN_SUB_ROWS = 512
XATTN_ROW_TILE = 1024
XATTN_SUB_ROWS = 512
XATTN_SHORT_ROWS = 256
COL_CHUNK = 512


def _vmem_limit(block_bytes, scratch_bytes, temp_bytes):
    est = 2 * block_bytes + scratch_bytes + temp_bytes
    return int(min(est + est // 8, V7X_VMEM_BYTES - 6 * 1024 * 1024))


def _nbytes(shape, dtype):
    n = 1
    for s in shape:
        n *= s
    return n * jnp.dtype(dtype).itemsize


def _rms(x, g):
    ms = jnp.mean(x * x, axis=-1, keepdims=True)
    return x * lax.rsqrt(ms + EPS) * g


def _dot(a, b):
    return jnp.dot(a, b, preferred_element_type=F32)


def _hist_pad(h):
    return -(-h // SUBLANES) * SUBLANES


def _post_residual(o_ref, x_ref, g_ref, scale, idx=Ellipsis):
    o = o_ref[idx]
    ms = jnp.mean(o * o, axis=-1, keepdims=True)
    o_ref[idx] = x_ref[idx] + o_ref[idx] * (scale * lax.rsqrt(ms + EPS)) * g_ref[...]


def _causal_depthwise(load, w, tt, start0):
    width = w.shape[0]
    out = None
    for s in range(SUBLANES):
        taps = [k for k in range(width) if (start0 + k) % SUBLANES == s]
        if not taps:
            continue
        rows = tt if s == 0 else tt + SUBLANES
        part = None
        for k in taps:
            off = start0 + k - s
            term = w[k:k + 1, :] * load(off, rows)
            part = term if part is None else part + term
        if s:
            part = part[:, s:s + tt, :]
        out = part if out is None else out + part
    return out


def _chunk_variants(c, nc, body):
    if nc == 1:
        body(True, True)
        return
    pl.when(c == 0)(lambda: body(True, False))
    if nc > 2:
        pl.when(jnp.logical_and(c > 0, c < nc - 1))(lambda: body(False, False))
    pl.when(c == nc - 1)(lambda: body(False, True))


def _ffn_kernel(x_ref, gpre_ref, gpost_ref, wg_ref, wu_ref, wd_ref, o_ref, h_ref, *, nj):
    j = pl.program_id(1)

    tm, d = x_ref.shape
    sub = min(tm, FFN_SUB_ROWS)

    def body(first, last):
        for r in range(tm // sub):
            rows = slice(r * sub, (r + 1) * sub)
            if first:
                h_ref[rows, :] = _rms(x_ref[rows, :], gpre_ref[...]).astype(BF16)
            h = h_ref[rows, :]
            gate = _dot(h, wg_ref[...])
            up = _dot(h, wu_ref[...])
            a = (gate * jax.nn.sigmoid(gate) * up).astype(BF16)
            if first:
                o_ref[rows, :] = _dot(a, wd_ref[...])
            else:
                o_ref[rows, :] += _dot(a, wd_ref[...])
            if last:
                _post_residual(o_ref, x_ref, gpost_ref, MACARON, (rows, slice(None)))

    _chunk_variants(j, nj, body)


def _ffn(x, g_pre, g_post, w_gu, w_down, layer, slot, gidx):
    n, d = x.shape
    dff = w_down.shape[2]
    tm = min(FFN_ROW_TILE, n)
    tf = min(COL_CHUNK, dff)
    nj = dff // tf
    assert n % tm == 0 and dff % tf == 0
    blocks = (_nbytes((tm, d), F32) * 2 + _nbytes((d, tf), BF16) * 3)
    scratch = _nbytes((tm, d), BF16)
    temps = _nbytes((min(tm, FFN_SUB_ROWS), tf), F32) * 6
    return pl.pallas_call(
        functools.partial(_ffn_kernel, nj=nj),
        grid=(n // tm, nj),
        in_specs=[
            pl.BlockSpec((tm, d), lambda i, j: (i, 0)),
            pl.BlockSpec((None, 1, d), lambda i, j: (gidx, 0, 0)),
            pl.BlockSpec((None, 1, d), lambda i, j: (gidx, 0, 0)),
            pl.BlockSpec((None, None, d, tf), lambda i, j: (layer, slot, 0, j)),
            pl.BlockSpec((None, None, d, tf), lambda i, j: (layer, slot, 0, nj + j)),
            pl.BlockSpec((None, None, tf, d), lambda i, j: (layer, slot, j, 0)),
        ],
        out_specs=pl.BlockSpec((tm, d), lambda i, j: (i, 0)),
        out_shape=jax.ShapeDtypeStruct((n, d), F32),
        scratch_shapes=[pltpu.VMEM((tm, d), BF16)],
        compiler_params=pltpu.CompilerParams(
            dimension_semantics=("parallel", "arbitrary"),
            vmem_limit_bytes=_vmem_limit(blocks, scratch, temps)),
        name="ffn",
    )(x, g_pre, g_post, w_gu, w_gu, w_down)


def _seq_tiles(b, t, rows=None):
    rows = ROW_TILE if rows is None else rows
    if t >= rows:
        assert t % rows == 0
        return 1, rows
    nb = min(b, max(1, rows // t))
    assert b % nb == 0 and t % SUBLANES == 0
    return nb, t


def _pool_kernel(x_ref, hist_ref, gpre_ref, gpost_ref, w_ref, scale_ref,
                 o_ref, nh_ref, ext_ref, *, pos0):
    t = pl.program_id(1)
    nt = pl.num_programs(1)
    nb, tt, d = x_ref.shape
    hp = POOL_PAD
    gc = d // len(POOL_WINDOWS)

    @pl.when(t == 0)
    def _():
        ext_ref[:, 0:hp, :] = jnp.zeros((nb, hp, d), F32)
        ext_ref[:, hp - POOL_HIST:hp, :] = hist_ref[...]

    @pl.when(t > 0)
    def _():
        ext_ref[:, hp - POOL_HIST:hp, :] = ext_ref[:, hp + tt - POOL_HIST:hp + tt, :]

    x = x_ref[...]
    ext_ref[:, hp:hp + tt, :] = _rms(x, gpre_ref[...])
    pos = pos0 + t * tt + lax.broadcasted_iota(jnp.int32, (1, tt, 1), 1)
    span_rows = tt + hp - SUBLANES
    ys = []
    for g, w in enumerate(POOL_WINDOWS):
        sl = slice(g * gc, (g + 1) * gc)
        u = ext_ref[:, hp:hp + tt, sl]
        win = (ext_ref[:, SUBLANES:SUBLANES + span_rows, sl]
               + ext_ref[:, SUBLANES - 1:SUBLANES - 1 + span_rows, sl])
        span, origin = 2, 0
        while span < w:
            n = win.shape[1]
            win = win[:, SUBLANES:, :] + win[:, SUBLANES - span:n - span, :]
            span, origin = 2 * span, origin + SUBLANES
        first = hp - SUBLANES - origin
        win = win[:, first:first + tt, :]
        inv_cnt = 1.0 / jnp.minimum(pos + 1, w).astype(F32)
        dg = (win * inv_cnt - u).reshape(nb * tt, gc).astype(BF16)
        ys.append(_dot(dg, w_ref[g]))
    y = jnp.concatenate(ys, axis=-1) * scale_ref[...]
    out = x.reshape(nb * tt, d) + _rms(y, gpost_ref[...])
    o_ref[...] = out.reshape(nb, tt, d)

    @pl.when(t == nt - 1)
    def _():
        nh_ref[...] = ext_ref[:, hp + tt - POOL_HIST:hp + tt, :]


def _pool_mixer(x, hist, g_pre, g_post, pool_w, pool_scale, slot, gidx, pos0):
    b, t, d = x.shape
    nb, tt = _seq_tiles(b, t)
    assert tt >= POOL_HIST
    ng = len(POOL_WINDOWS)
    gc = d // ng
    hp = POOL_PAD
    blocks = _nbytes((nb, tt, d), F32) * 2 + _nbytes((ng, gc, gc), BF16)
    scratch = _nbytes((nb, hp + tt, d), F32)
    temps = _nbytes((nb, tt, d), F32) * 4
    return pl.pallas_call(
        functools.partial(_pool_kernel, pos0=pos0),
        grid=(b // nb, t // tt),
        in_specs=[
            pl.BlockSpec((nb, tt, d), lambda i, j: (i, j, 0)),
            pl.BlockSpec((nb, POOL_HIST, d), lambda i, j: (i, 0, 0)),
            pl.BlockSpec((None, 1, d), lambda i, j: (gidx, 0, 0)),
            pl.BlockSpec((None, 1, d), lambda i, j: (gidx, 0, 0)),
            pl.BlockSpec((None, ng, gc, gc), lambda i, j: (slot, 0, 0, 0)),
            pl.BlockSpec((None, 1, d), lambda i, j: (slot, 0, 0)),
        ],
        out_specs=[
            pl.BlockSpec((nb, tt, d), lambda i, j: (i, j, 0)),
            pl.BlockSpec((nb, POOL_HIST, d), lambda i, j: (i, 0, 0)),
        ],
        out_shape=[jax.ShapeDtypeStruct((b, t, d), F32),
                   jax.ShapeDtypeStruct((b, POOL_HIST, d), F32)],
        scratch_shapes=[pltpu.VMEM((nb, hp + tt, d), F32)],
        compiler_params=pltpu.CompilerParams(
            dimension_semantics=("parallel", "arbitrary"),
            vmem_limit_bytes=_vmem_limit(blocks, scratch, temps)),
        name="pool_mixer",
    )(x, hist, g_pre, g_post, pool_w, pool_scale)


def _sconv_kernel(x_ref, hist_ref, gpre_ref, gpost_ref, wb_ref, wc_ref, wh_ref,
                  wconv_ref, wout_ref, o_ref, nh_ref, u_ref, carry_ref, ext_ref, *, nc):
    t = pl.program_id(1)
    nt = pl.num_programs(1)
    c = pl.program_id(2)
    nb, tt, d = x_ref.shape
    tn = wb_ref.shape[-1]
    width = wconv_ref.shape[0]
    nh = width - 1
    hp = _hist_pad(nh)

    @pl.when(t == 0)
    def _():
        ext_ref[:, 0:hp, :] = jnp.zeros((nb, hp, tn), F32)
        ext_ref[:, hp - nh:hp, :] = hist_ref[...]

    @pl.when(t > 0)
    def _():
        ext_ref[:, hp - nh:hp, :] = carry_ref[c, :, 0:nh, :]

    def body(first, last):
        if first:
            u_ref[...] = _rms(x_ref[...], gpre_ref[...]).reshape(nb * tt, d).astype(BF16)
        u = u_ref[...]
        gb = _dot(u, wb_ref[...])
        v = (_dot(u, wc_ref[...]) * _dot(u, wh_ref[...])).reshape(nb, tt, tn)
        ext_ref[:, hp:hp + tt, :] = v
        carry_ref[c, :, 0:nh, :] = v[:, tt - nh:tt, :]
        conv = _causal_depthwise(lambda off, rows: ext_ref[:, off:off + rows, :],
                                 wconv_ref[...], tt, hp - nh)
        y =(gb.reshape(nb, tt, tn) * conv).reshape(nb * tt, tn).astype(BF16)
        if first:
            o_ref[...] = _dot(y, wout_ref[...]).reshape(nb, tt, d)
        else:
            o_ref[...] += _dot(y, wout_ref[...]).reshape(nb, tt, d)
        if last:
            _post_residual(o_ref, x_ref, gpost_ref, 1.0)

    _chunk_variants(c, nc, body)

    @pl.when(jnp.logical_and(t == nt - 1, c == nc - 1))
    def _():
        for k in range(nc):
            nh_ref[:, :, k * tn:(k + 1) * tn] = carry_ref[k, :, 0:nh, :]


def _sconv_mixer(x, hist, g_pre, g_post, w_in, w_conv, w_out, slot, gidx):
    b, t, d = x.shape
    nb, tt = _seq_tiles(b, t)
    width = w_conv.shape[1]
    nh = width - 1
    assert tt >= nh
    tn = min(COL_CHUNK, d)
    nc = d // tn
    hp = _hist_pad(nh)
    blocks = _nbytes((nb, tt, d), F32) * 2 + _nbytes((d, tn), BF16) * 4
    scratch = (_nbytes((nb * tt, d), BF16) + _nbytes((nc, nb, SUBLANES, tn), F32)
               + _nbytes((nb, hp + tt, tn), F32))
    temps = _nbytes((nb * tt, tn), F32) * 6 + _nbytes((nb * tt, d), F32)
    return pl.pallas_call(
        functools.partial(_sconv_kernel, nc=nc),
        grid=(b // nb, t // tt, nc),
        in_specs=[
            pl.BlockSpec((nb, tt, d), lambda i, j, c: (i, j, 0)),
            pl.BlockSpec((nb, nh, tn), lambda i, j, c: (i, 0, c)),
            pl.BlockSpec((None, 1, d), lambda i, j, c: (gidx, 0, 0)),
            pl.BlockSpec((None, 1, d), lambda i, j, c: (gidx, 0, 0)),
            pl.BlockSpec((None, d, tn), lambda i, j, c: (slot, 0, c)),
            pl.BlockSpec((None, d, tn), lambda i, j, c: (slot, 0, nc + c)),
            pl.BlockSpec((None, d, tn), lambda i, j, c: (slot, 0, 2 * nc + c)),
            pl.BlockSpec((None, width, tn), lambda i, j, c: (slot, 0, c)),
            pl.BlockSpec((None, tn, d), lambda i, j, c: (slot, c, 0)),
        ],
        out_specs=[
            pl.BlockSpec((nb, tt, d), lambda i, j, c: (i, j, 0)),
            pl.BlockSpec((nb, nh, d), lambda i, j, c: (i, 0, 0)),
        ],
        out_shape=[jax.ShapeDtypeStruct((b, t, d), F32),
                   jax.ShapeDtypeStruct((b, nh, d), F32)],
        scratch_shapes=[pltpu.VMEM((nb * tt, d), BF16),
                        pltpu.VMEM((nc, nb, SUBLANES, tn), F32),
                        pltpu.VMEM((nb, hp + tt, tn), F32)],
        compiler_params=pltpu.CompilerParams(
            dimension_semantics=("parallel", "arbitrary", "arbitrary"),
            vmem_limit_bytes=_vmem_limit(blocks, scratch, temps)),
        name="sconv_mixer",
    )(x, hist, g_pre, g_post, w_in, w_in, w_in, w_conv, w_out)


def _cconv_kernel(x_ref, hist_ref, gpre_ref, gpost_ref, wa_ref, wg_ref, ba_ref, bg_ref,
                  wdw_ref, bdw_ref, lng_ref, lnb_ref, w2_ref, b2_ref,
                  o_ref, nh_ref, u_ref, hc_ref, carry_ref, ext_ref, *, nc):
    t = pl.program_id(1)
    nt = pl.num_programs(1)
    c = pl.program_id(2)
    nb, tt, d = x_ref.shape
    tn = wa_ref.shape[-1]
    width = wdw_ref.shape[0]
    nh = width - 1
    hp = _hist_pad(nh)
    m = nb * tt

    @pl.when(jnp.logical_and(c < nc, t == 0))
    def _():
        ext_ref[:, 0:hp, :] = jnp.zeros((nb, hp, tn), F32)
        ext_ref[:, hp - nh:hp, :] = hist_ref[...]

    @pl.when(jnp.logical_and(c < nc, t > 0))
    def _():
        ext_ref[:, hp - nh:hp, :] = carry_ref[c, :, 0:nh, :]

    def project_and_conv(first):
        if first:
            u_ref[...] = _rms(x_ref[...], gpre_ref[...]).reshape(m, d).astype(BF16)
        u = u_ref[...]
        a = _dot(u, wa_ref[...]) + ba_ref[...]
        gate = _dot(u, wg_ref[...]) + bg_ref[...]
        v = (a * jax.nn.sigmoid(gate)).reshape(nb, tt, tn)
        ext_ref[:, hp:hp + tt, :] = v
        carry_ref[c, :, 0:nh, :] = v[:, tt - nh:tt, :]
        conv = _causal_depthwise(lambda off, rows: ext_ref[:, off:off + rows, :],
                                 wdw_ref[...], tt, hp - nh) + bdw_ref[...]
        hc_ref[c] = conv.reshape(m, tn)

    pl.when(c == 0)(lambda: project_and_conv(True))
    if nc > 1:
        pl.when(jnp.logical_and(c > 0, c < nc))(lambda: project_and_conv(False))

    def layernorm_silu():
        s1 = jnp.zeros((m, 1), F32)
        for k in range(nc):
            s1 = s1 + jnp.sum(hc_ref[k], axis=-1, keepdims=True)
        mean = s1 / d
        s2 = jnp.zeros((m, 1), F32)
        for k in range(nc):
            xc = hc_ref[k] - mean
            s2 = s2 + jnp.sum(xc * xc, axis=-1, keepdims=True)
        rstd = lax.rsqrt(s2 / d + EPS)
        for k in range(nc):
            sl = slice(k * tn, (k + 1) * tn)
            hn = (hc_ref[k] - mean) * rstd * lng_ref[:, sl] + lnb_ref[:, sl]
            u_ref[:, sl] = (hn * jax.nn.sigmoid(hn)).astype(BF16)

    for k in range(nc):
        @pl.when(c == nc + k)
        def _(k=k):
            if k == 0:
                layernorm_silu()
            sl = slice(k * tn, (k + 1) * tn)
            y = _dot(u_ref[...], w2_ref[...]) + b2_ref[:, sl]
            o_ref[:, :, sl] = y.reshape(nb, tt, tn)
            if k == nc - 1:
                _post_residual(o_ref, x_ref, gpost_ref, 1.0)

    @pl.when(jnp.logical_and(t == nt - 1, c == 2 * nc - 1))
    def _():
        for k in range(nc):
            nh_ref[:, :, k * tn:(k + 1) * tn] = carry_ref[k, :, 0:nh, :]


def _cconv_mixer(x, hist, g_pre, g_post, w_pw1, b_pw1, w_dw, b_dw, ln_g, ln_b,
                 w_pw2, b_pw2, slot, gidx):
    b, t, d = x.shape
    nb, tt = _seq_tiles(b, t)
    width = w_dw.shape[1]
    nh = width - 1
    assert tt >= nh
    tn = min(COL_CHUNK, d)
    nc = d // tn
    hp = _hist_pad(nh)
    m = nb * tt

    def p1(c):
        return jnp.minimum(c, nc - 1)

    def p2(c):
        return jnp.maximum(c - nc, 0)

    blocks = (_nbytes((nb, tt, d), F32) * 2 + _nbytes((d, tn), BF16) * 3
              + _nbytes((nb, nh, tn), F32) * 2)
    scratch = (_nbytes((m, d), BF16) + _nbytes((nc, m, tn), F32)
               + _nbytes((nc, nb, hp, tn), F32) + _nbytes((nb, hp + tt, tn), F32))
    temps = _nbytes((m, tn), F32) * 6 + _nbytes((m, d), F32)
    return pl.pallas_call(
        functools.partial(_cconv_kernel, nc=nc),
        grid=(b // nb, t // tt, 2 * nc),
        in_specs=[
            pl.BlockSpec((nb, tt, d), lambda i, j, c: (i, j, 0)),
            pl.BlockSpec((nb, nh, tn), lambda i, j, c: (i, 0, p1(c))),
            pl.BlockSpec((None, 1, d), lambda i, j, c: (gidx, 0, 0)),
            pl.BlockSpec((None, 1, d), lambda i, j, c: (gidx, 0, 0)),
            pl.BlockSpec((None, d, tn), lambda i, j, c: (slot, 0, p1(c))),
            pl.BlockSpec((None, d, tn), lambda i, j, c: (slot, 0, nc + p1(c))),
            pl.BlockSpec((None, 1, tn), lambda i, j, c: (slot, 0, p1(c))),
            pl.BlockSpec((None, 1, tn), lambda i, j, c: (slot, 0, nc + p1(c))),
            pl.BlockSpec((None, width, tn), lambda i, j, c: (slot, 0, p1(c))),
            pl.BlockSpec((None, 1, tn), lambda i, j, c: (slot, 0, p1(c))),
            pl.BlockSpec((None, 1, d), lambda i, j, c: (slot, 0, 0)),
            pl.BlockSpec((None, 1, d), lambda i, j, c: (slot, 0, 0)),
            pl.BlockSpec((None, d, tn), lambda i, j, c: (slot, 0, p2(c))),
            pl.BlockSpec((None, 1, d), lambda i, j, c: (slot, 0, 0)),
        ],
        out_specs=[
            pl.BlockSpec((nb, tt, d), lambda i, j, c: (i, j, 0)),
            pl.BlockSpec((nb, nh, d), lambda i, j, c: (i, 0, 0)),
        ],
        out_shape=[jax.ShapeDtypeStruct((b, t, d), F32),
                   jax.ShapeDtypeStruct((b, nh, d), F32)],
        scratch_shapes=[pltpu.VMEM((m, d), BF16),
                        pltpu.VMEM((nc, m, tn), F32),
                        pltpu.VMEM((nc, nb, hp, tn), F32),
                        pltpu.VMEM((nb, hp + tt, tn), F32)],
        compiler_params=pltpu.CompilerParams(
            dimension_semantics=("parallel", "arbitrary", "arbitrary"),
            vmem_limit_bytes=_vmem_limit(blocks, scratch, temps)),
        name="cconv_mixer",
    )(x, hist, g_pre, g_post, w_pw1, w_pw1, b_pw1, b_pw1, w_dw, b_dw, ln_g, ln_b,
      w_pw2, b_pw2)


def _xattn_kernel(x_ref, k_ref, v_ref, gpre_ref, gpost_ref, wq_ref, wo_ref,
                  o_ref, u_ref, *, nheads, tt):
    h = pl.program_id(2)
    m, d = x_ref.shape
    nb = k_ref.shape[0]
    hd = wq_ref.shape[-1]
    sub = min(m, XATTN_SUB_ROWS)
    seq_rows = min(sub, tt)

    def body(first, last):
        for r in range(m // sub):
            rows = slice(r * sub, (r + 1) * sub)
            if first:
                u_ref[rows, :] = _rms(x_ref[rows, :], gpre_ref[...]).astype(BF16)
            q = _dot(u_ref[rows, :], wq_ref[...]).astype(BF16)
            outs = []
            for s0 in range(0, sub, seq_rows):
                b = (r * sub + s0) // tt if nb > 1 else 0
                kb = k_ref[b].astype(BF16)
                vb = v_ref[b].astype(BF16)
                s = lax.dot_general(q[s0:s0 + seq_rows, :], kb, (((1,), (1,)), ((), ())),
                                    preferred_element_type=F32) * (hd ** -0.5)
                e = jnp.exp(s - jnp.max(s, axis=-1, keepdims=True))
                p = (e / jnp.sum(e, axis=-1, keepdims=True)).astype(BF16)
                outs.append(_dot(p, vb).astype(BF16))
            o = outs[0] if len(outs) == 1 else jnp.concatenate(outs, axis=0)
            if first:
                o_ref[rows, :] = _dot(o, wo_ref[...])
            else:
                o_ref[rows, :] += _dot(o, wo_ref[...])
            if last:
                _post_residual(o_ref, x_ref, gpost_ref, 1.0, (rows, slice(None)))

    _chunk_variants(h, nheads, body)


def _xattn(x, mem_k, mem_v, g_pre, g_post, w_q, w_o, nheads, layer, gidx):
    b, t, d = x.shape
    nb, tt = _seq_tiles(b, t, XATTN_ROW_TILE if t >= XATTN_ROW_TILE else XATTN_SHORT_ROWS)
    n_mem = mem_k.shape[2]
    hd = d // nheads
    m = nb * tt
    nt = t // tt
    sub = min(m, XATTN_SUB_ROWS)
    assert m % sub == 0 and (sub % tt == 0 or tt % sub == 0)
    blocks = (_nbytes((m, d), F32) * 2 + _nbytes((nb, n_mem, hd), F32) * 2
              + _nbytes((d, hd), BF16) * 2)
    scratch = _nbytes((m, d), BF16)
    temps = _nbytes((sub, hd), F32) * 4 + _nbytes((sub, n_mem), F32) * 4
    out = pl.pallas_call(
        functools.partial(_xattn_kernel, nheads=nheads, tt=tt),
        grid=(b // nb, nt, nheads),
        in_specs=[
            pl.BlockSpec((m, d), lambda i, j, h: (i * nt + j, 0)),
            pl.BlockSpec((None, nb, n_mem, hd), lambda i, j, h: (layer, i, 0, h)),
            pl.BlockSpec((None, nb, n_mem, hd), lambda i, j, h: (layer, i, 0, h)),
            pl.BlockSpec((None, 1, d), lambda i, j, h: (gidx, 0, 0)),
            pl.BlockSpec((None, 1, d), lambda i, j, h: (gidx, 0, 0)),
            pl.BlockSpec((None, d, hd), lambda i, j, h: (layer, 0, h)),
            pl.BlockSpec((None, hd, d), lambda i, j, h: (layer, h, 0)),
        ],
        out_specs=pl.BlockSpec((m, d), lambda i, j, h: (i * nt + j, 0)),
        out_shape=jax.ShapeDtypeStruct((b * t, d), F32),
        scratch_shapes=[pltpu.VMEM((m, d), BF16)],
        compiler_params=pltpu.CompilerParams(
            dimension_semantics=("parallel", "arbitrary", "arbitrary"),
            vmem_limit_bytes=_vmem_limit(blocks, scratch, temps)),
        name="xattn",
    )(x.reshape(b * t, d), mem_k, mem_v, g_pre, g_post, w_q, w_o)
    return out.reshape(b, t, d)


def _memkv_kernel(mem_ref, g_ref, wk_ref, wv_ref, k_ref, v_ref, kb_ref, vb_ref, m_ref,
                  *, nheads):
    c = pl.program_id(2)

    @pl.when(c == 0)
    def _():
        m_ref[...] = _rms(mem_ref[...], g_ref[...]).astype(BF16)

    mm = m_ref[...]
    k = _dot(mm, wk_ref[...])
    v = _dot(mm, wv_ref[...])
    kb_ref[...] = k.astype(BF16)
    vb_ref[...] = v.astype(BF16)
    for hh in range(nheads):
        @pl.when(c == hh)
        def _(hh=hh):
            k_ref[:, hh, :] = k
            v_ref[:, hh, :] = v


def _mem_kv(mem, g_mem, w_k, w_v, nheads):
    n, d = mem.shape
    depth = w_k.shape[0]
    hd = d // nheads
    tm = min(ROW_TILE, n)
    assert n % tm == 0
    padded_heads = -(-nheads // SUBLANES) * SUBLANES
    blocks = (_nbytes((tm, d), F32) + _nbytes((d, hd), BF16) * 2
              + _nbytes((tm, padded_heads, hd), F32) * 2 + _nbytes((tm, hd), BF16) * 2)
    scratch = _nbytes((tm, d), BF16)
    temps = _nbytes((tm, hd), F32) * 4
    out = jax.ShapeDtypeStruct((depth, n, nheads, hd), F32)
    out_bf = jax.ShapeDtypeStruct((depth, n, d), BF16)
    return pl.pallas_call(
        functools.partial(_memkv_kernel, nheads=nheads),
        grid=(depth, n // tm, nheads),
        in_specs=[
            pl.BlockSpec((tm, d), lambda l, i, c: (i, 0)),
            pl.BlockSpec((None, 1, d), lambda l, i, c: (l, 0, 0)),
            pl.BlockSpec((None, d, hd), lambda l, i, c: (l, 0, c)),
            pl.BlockSpec((None, d, hd), lambda l, i, c: (l, 0, c)),
        ],
        out_specs=[pl.BlockSpec((None, tm, nheads, hd), lambda l, i, c: (l, i, 0, 0)),
                   pl.BlockSpec((None, tm, nheads, hd), lambda l, i, c: (l, i, 0, 0)),
                   pl.BlockSpec((None, tm, hd), lambda l, i, c: (l, i, c)),
                   pl.BlockSpec((None, tm, hd), lambda l, i, c: (l, i, c))],
        out_shape=[out, out, out_bf, out_bf],
        scratch_shapes=[pltpu.VMEM((tm, d), BF16)],
        compiler_params=pltpu.CompilerParams(
            dimension_semantics=("parallel", "parallel", "arbitrary"),
            vmem_limit_bytes=_vmem_limit(blocks, scratch, temps)),
        name="mem_kv",
    )(mem, g_mem, w_k, w_v)


def kernel(x_prompt, x_sample, mem_prompt, cache_pool, cache_sconv, cache_cconv, cache_mem_k, cache_mem_v, g_pre, g_post, g_mem, ffn_w_gu, ffn_w_down, xa_w_q, xa_w_k, xa_w_v, xa_w_o, pool_w, pool_scale, sc_w_in, sc_w_conv, sc_w_out, cc_w_pw1, cc_b_pw1, cc_w_dw, cc_b_dw, cc_ln_g, cc_ln_b, cc_w_pw2, cc_b_pw2):
    depth = g_pre.shape[0]
    d = x_prompt.shape[-1]
    bp = x_prompt.shape[0]
    n_mem = mem_prompt.shape[1]
    heads, head_dim = cache_mem_k.shape[3], cache_mem_k.shape[4]

    w_gu, w_down = ffn_w_gu.astype(BF16), ffn_w_down.astype(BF16)
    w_q, w_k, w_v, w_o = (w.astype(BF16) for w in (xa_w_q, xa_w_k, xa_w_v, xa_w_o))
    w_pool = pool_w.astype(BF16)
    w_sc_in, w_sc_out = sc_w_in.astype(BF16), sc_w_out.astype(BF16)
    w_pw1, w_pw2 = cc_w_pw1.astype(BF16), cc_w_pw2.astype(BF16)

    def rows(a):
        return a.reshape(-1, 1, a.shape[-1])

    gp, gq = rows(g_pre), rows(g_post)
    gm = rows(g_mem)
    p_scale = rows(pool_scale)
    b_pw1, b_dw, ln_g, ln_b, b_pw2 = (rows(a) for a in (cc_b_pw1, cc_b_dw, cc_ln_g, cc_ln_b, cc_b_pw2))

    mk, mv, mk_bf, mv_bf = _mem_kv(mem_prompt.reshape(bp * n_mem, d), gm, w_k, w_v, heads)
    mem_k_p = mk_bf.reshape(depth, bp, n_mem, d)
    mem_v_p = mv_bf.reshape(depth, bp, n_mem, d)
    mem_k_s = cache_mem_k.reshape(depth, -1, n_mem, d)
    mem_v_s = cache_mem_v.reshape(depth, -1, n_mem, d)

    def trunk(x, pos0, pool_hist, sconv_hist, cconv_hist, mem_k, mem_v):
        b, t, _ = x.shape
        new_pool, new_sconv, new_cconv = [], [], []
        for i in range(depth):
            x = _ffn(x.reshape(b * t, d), gp, gq, w_gu, w_down, i, 0, 4 * i).reshape(b, t, d)
            kind, slot = i % N_MIXERS, i // N_MIXERS
            if kind == 0:
                x, st = _pool_mixer(x, pool_hist[slot], gp, gq, w_pool, p_scale, slot, 4 * i + 1, pos0)
                new_pool.append(st)
            elif kind == 1:
                x, st = _sconv_mixer(x, sconv_hist[slot], gp, gq, w_sc_in, sc_w_conv, w_sc_out,
                                     slot, 4 * i + 1)
                new_sconv.append(st)
            else:
                x, st = _cconv_mixer(x, cconv_hist[slot], gp, gq, w_pw1, b_pw1, cc_w_dw, b_dw,
                                     ln_g, ln_b, w_pw2, b_pw2, slot, 4 * i + 1)
                new_cconv.append(st)
            x = _xattn(x, mem_k, mem_v, gp, gq, w_q, w_o, heads, i, 4 * i + 2)
            x = _ffn(x.reshape(b * t, d), gp, gq, w_gu, w_down, i, 1, 4 * i + 3).reshape(b, t, d)
        return x, jnp.stack(new_pool), jnp.stack(new_sconv), jnp.stack(new_cconv)

    dt = x_prompt.dtype
    n_pool, n_sconv, n_cconv = cache_pool.shape[0], cache_sconv.shape[0], cache_cconv.shape[0]
    pool_h0 = [jnp.zeros((bp, POOL_HIST, d), dt)] * n_pool
    sconv_h0 = [jnp.zeros((bp, cache_sconv.shape[2], d), dt)] * n_sconv
    cconv_h0 = [jnp.zeros((bp, cache_cconv.shape[2], d), dt)] * n_cconv
    y_prompt, pool_p, sconv_p, cconv_p = trunk(x_prompt, 0, pool_h0, sconv_h0, cconv_h0, mem_k_p, mem_v_p)
    y_sample, pool_s, sconv_s, cconv_s = trunk(
        x_sample, PAST_LEN,
        [cache_pool[j] for j in range(n_pool)],
        [cache_sconv[j] for j in range(n_sconv)],
        [cache_cconv[j] for j in range(n_cconv)],
        mem_k_s, mem_v_s)

    mem_k_prompt = mk.reshape(depth, bp, n_mem, heads, head_dim)
    mem_v_prompt = mv.reshape(depth, bp, n_mem, heads, head_dim)
    return (y_prompt, y_sample, pool_p, pool_s, sconv_p, sconv_s, cconv_p, cconv_s,
            mem_k_prompt, mem_v_prompt)
```

```python
import functools

import jax
import jax.numpy as jnp
from jax import lax
from jax.experimental import pallas as pl
from jax.experimental.pallas import tpu as pltpu

EPS = 1e-6
MACARON = 0.5
PAST_LEN = 1024
POOL_WINDOWS = (2, 4, 8, 16)
POOL_HIST = max(POOL_WINDOWS) - 1
N_MIXERS = 3

BF16 = jnp.bfloat16
F32 = jnp.float32

V7X_VMEM_BYTES = 64 * 1024 * 1024
SUBLANES = 8

assert all(w & (w - 1) == 0 for w in POOL_WINDOWS)
POOL_PAD = SUBLANES * (max(POOL_WINDOWS).bit_length() - 1)
assert POOL_PAD >= POOL_HIST + 1

ROW_TILE = 512
FFN_ROW_TILE = 1024
FFN_SUB_ROWS = 512
XATTN_ROW_TILE = 1024
XATTN_SUB_ROWS = 512
XATTN_HEADS_PER_STEP = 2
XATTN_SHORT_ROWS = 256
COL_CHUNK = 512


def _vmem_limit(block_bytes, scratch_bytes, temp_bytes):
    est = 2 * block_bytes + scratch_bytes + temp_bytes
    return int(min(est + est // 8, V7X_VMEM_BYTES - 6 * 1024 * 1024))


def _nbytes(shape, dtype):
    n = 1
    for s in shape:
        n *= s
    return n * jnp.dtype(dtype).itemsize


def _rms(x, g):
    ms = jnp.mean(x * x, axis=-1, keepdims=True)
    return x * lax.rsqrt(ms + EPS) * g


def _dot(a, b):
    return jnp.dot(a, b, preferred_element_type=F32)


def _hist_pad(h):
    return -(-h // SUBLANES) * SUBLANES


def _post_residual(o_ref, x_ref, g_ref, scale, idx=Ellipsis):
    o = o_ref[idx]
    ms = jnp.mean(o * o, axis=-1, keepdims=True)
    o_ref[idx] = x_ref[idx] + o_ref[idx] * (scale * lax.rsqrt(ms + EPS)) * g_ref[...]


def _causal_depthwise(load, w, tt, start0):
    width = w.shape[0]
    out = None
    for s in range(SUBLANES):
        taps = [k for k in range(width) if (start0 + k) % SUBLANES == s]
        if not taps:
            continue
        rows = tt if s == 0 else tt + SUBLANES
        part = None
        for k in taps:
            off = start0 + k - s
            term = w[k:k + 1, :] * load(off, rows)
            part = term if part is None else part + term
        if s:
            part = part[:, s:s + tt, :]
        out = part if out is None else out + part
    return out


def _chunk_variants(c, nc, body):
    if nc == 1:
        body(True, True)
        return
    pl.when(c == 0)(lambda: body(True, False))
    if nc > 2:
        pl.when(jnp.logical_and(c > 0, c < nc - 1))(lambda: body(False, False))
    pl.when(c == nc - 1)(lambda: body(False, True))


def _ffn_kernel(x_ref, gpre_ref, gpost_ref, wg_ref, wu_ref, wd_ref, o_ref, h_ref, *, nj):
    j = pl.program_id(1)

    tm, d = x_ref.shape
    sub = min(tm, FFN_SUB_ROWS)

    def body(first, last):
        for r in range(tm // sub):
            rows = slice(r * sub, (r + 1) * sub)
            if first:
                h_ref[rows, :] = _rms(x_ref[rows, :], gpre_ref[...]).astype(BF16)
            h = h_ref[rows, :]
            gate = _dot(h, wg_ref[...])
            up = _dot(h, wu_ref[...])
            a = (gate * jax.nn.sigmoid(gate) * up).astype(BF16)
            if first:
                o_ref[rows, :] = _dot(a, wd_ref[...])
            else:
                o_ref[rows, :] += _dot(a, wd_ref[...])
            if last:
                _post_residual(o_ref, x_ref, gpost_ref, MACARON, (rows, slice(None)))

    _chunk_variants(j, nj, body)


def _ffn(x, g_pre, g_post, w_gu, w_down, layer, slot, gidx):
    n, d = x.shape
    dff = w_down.shape[2]
    tm = min(FFN_ROW_TILE, n)
    tf = min(COL_CHUNK, dff)
    nj = dff // tf
    assert n % tm == 0 and dff % tf == 0
    blocks = (_nbytes((tm, d), F32) * 2 + _nbytes((d, tf), BF16) * 3)
    scratch = _nbytes((tm, d), BF16)
    temps = _nbytes((min(tm, FFN_SUB_ROWS), tf), F32) * 6
    return pl.pallas_call(
        functools.partial(_ffn_kernel, nj=nj),
        grid=(n // tm, nj),
        in_specs=[
            pl.BlockSpec((tm, d), lambda i, j: (i, 0)),
            pl.BlockSpec((None, 1, d), lambda i, j: (gidx, 0, 0)),
            pl.BlockSpec((None, 1, d), lambda i, j: (gidx, 0, 0)),
            pl.BlockSpec((None, None, d, tf), lambda i, j: (layer, slot, 0, j)),
            pl.BlockSpec((None, None, d, tf), lambda i, j: (layer, slot, 0, nj + j)),
            pl.BlockSpec((None, None, tf, d), lambda i, j: (layer, slot, j, 0)),
        ],
        out_specs=pl.BlockSpec((tm, d), lambda i, j: (i, 0)),
        out_shape=jax.ShapeDtypeStruct((n, d), F32),
        scratch_shapes=[pltpu.VMEM((tm, d), BF16)],
        compiler_params=pltpu.CompilerParams(
            dimension_semantics=("parallel", "arbitrary"),
            vmem_limit_bytes=_vmem_limit(blocks, scratch, temps)),
        name="ffn",
    )(x, g_pre, g_post, w_gu, w_gu, w_down)


def _seq_tiles(b, t, rows=None):
    rows = ROW_TILE if rows is None else rows
    if t >= rows:
        assert t % rows == 0
        return 1, rows
    nb = min(b, max(1, rows // t))
    assert b % nb == 0 and t % SUBLANES == 0
    return nb, t


def _pool_kernel(x_ref, hist_ref, gpre_ref, gpost_ref, w_ref, scale_ref,
                 o_ref, nh_ref, ext_ref, *, pos0):
    t = pl.program_id(1)
    nt = pl.num_programs(1)
    nb, tt, d = x_ref.shape
    hp = POOL_PAD
    gc = d // len(POOL_WINDOWS)

    @pl.when(t == 0)
    def _():
        ext_ref[:, 0:hp, :] = jnp.zeros((nb, hp, d), F32)
        ext_ref[:, hp - POOL_HIST:hp, :] = hist_ref[...]

    @pl.when(t > 0)
    def _():
        ext_ref[:, hp - POOL_HIST:hp, :] = ext_ref[:, hp + tt - POOL_HIST:hp + tt, :]

    x = x_ref[...]
    ext_ref[:, hp:hp + tt, :] = _rms(x, gpre_ref[...])
    pos = pos0 + t * tt + lax.broadcasted_iota(jnp.int32, (1, tt, 1), 1)
    span_rows = tt + hp - SUBLANES
    ys = []
    for g, w in enumerate(POOL_WINDOWS):
        sl = slice(g * gc, (g + 1) * gc)
        u = ext_ref[:, hp:hp + tt, sl]
        win = (ext_ref[:, SUBLANES:SUBLANES + span_rows, sl]
               + ext_ref[:, SUBLANES - 1:SUBLANES - 1 + span_rows, sl])
        span, origin = 2, 0
        while span < w:
            n = win.shape[1]
            win = win[:, SUBLANES:, :] + win[:, SUBLANES - span:n - span, :]
            span, origin = 2 * span, origin + SUBLANES
        first = hp - SUBLANES - origin
        win = win[:, first:first + tt, :]
        inv_cnt = 1.0 / jnp.minimum(pos + 1, w).astype(F32)
        dg = (win * inv_cnt - u).reshape(nb * tt, gc).astype(BF16)
        ys.append(_dot(dg, w_ref[g]))
    y = jnp.concatenate(ys, axis=-1) * scale_ref[...]
    out = x.reshape(nb * tt, d) + _rms(y, gpost_ref[...])
    o_ref[...] = out.reshape(nb, tt, d)

    @pl.when(t == nt - 1)
    def _():
        nh_ref[...] = ext_ref[:, hp + tt - POOL_HIST:hp + tt, :]


def _pool_mixer(x, hist, g_pre, g_post, pool_w, pool_scale, slot, gidx, pos0):
    b, t, d = x.shape
    nb, tt = _seq_tiles(b, t)
    assert tt >= POOL_HIST
    ng = len(POOL_WINDOWS)
    gc = d // ng
    hp = POOL_PAD
    blocks = _nbytes((nb, tt, d), F32) * 2 + _nbytes((ng, gc, gc), BF16)
    scratch = _nbytes((nb, hp + tt, d), F32)
    temps = _nbytes((nb, tt, d), F32) * 4
    return pl.pallas_call(
        functools.partial(_pool_kernel, pos0=pos0),
        grid=(b // nb, t // tt),
        in_specs=[
            pl.BlockSpec((nb, tt, d), lambda i, j: (i, j, 0)),
            pl.BlockSpec((nb, POOL_HIST, d), lambda i, j: (i, 0, 0)),
            pl.BlockSpec((None, 1, d), lambda i, j: (gidx, 0, 0)),
            pl.BlockSpec((None, 1, d), lambda i, j: (gidx, 0, 0)),
            pl.BlockSpec((None, ng, gc, gc), lambda i, j: (slot, 0, 0, 0)),
            pl.BlockSpec((None, 1, d), lambda i, j: (slot, 0, 0)),
        ],
        out_specs=[
            pl.BlockSpec((nb, tt, d), lambda i, j: (i, j, 0)),
            pl.BlockSpec((nb, POOL_HIST, d), lambda i, j: (i, 0, 0)),
        ],
        out_shape=[jax.ShapeDtypeStruct((b, t, d), F32),
                   jax.ShapeDtypeStruct((b, POOL_HIST, d), F32)],
        scratch_shapes=[pltpu.VMEM((nb, hp + tt, d), F32)],
        compiler_params=pltpu.CompilerParams(
            dimension_semantics=("parallel", "arbitrary"),
            vmem_limit_bytes=_vmem_limit(blocks, scratch, temps)),
        name="pool_mixer",
    )(x, hist, g_pre, g_post, pool_w, pool_scale)


def _sconv_kernel(x_ref, hist_ref, gpre_ref, gpost_ref, wb_ref, wc_ref, wh_ref,
                  wconv_ref, wout_ref, o_ref, nh_ref, u_ref, carry_ref, ext_ref, *, nc):
    t = pl.program_id(1)
    nt = pl.num_programs(1)
    c = pl.program_id(2)
    nb, tt, d = x_ref.shape
    tn = wb_ref.shape[-1]
    width = wconv_ref.shape[0]
    nh = width - 1
    hp = _hist_pad(nh)

    @pl.when(t == 0)
    def _():
        ext_ref[:, 0:hp, :] = jnp.zeros((nb, hp, tn), F32)
        ext_ref[:, hp - nh:hp, :] = hist_ref[...]

    @pl.when(t > 0)
    def _():
        ext_ref[:, hp - nh:hp, :] = carry_ref[c, :, 0:nh, :]

    def body(first, last):
        if first:
            u_ref[...] = _rms(x_ref[...], gpre_ref[...]).reshape(nb * tt, d).astype(BF16)
        u = u_ref[...]
        gb = _dot(u, wb_ref[...])
        v = (_dot(u, wc_ref[...]) * _dot(u, wh_ref[...])).reshape(nb, tt, tn)
        ext_ref[:, hp:hp + tt, :] = v
        carry_ref[c, :, 0:nh, :] = v[:, tt - nh:tt, :]
        conv = _causal_depthwise(lambda off, rows: ext_ref[:, off:off + rows, :],
                                 wconv_ref[...], tt, hp - nh)
        y =(gb.reshape(nb, tt, tn) * conv).reshape(nb * tt, tn).astype(BF16)
        if first:
            o_ref[...] = _dot(y, wout_ref[...]).reshape(nb, tt, d)
        else:
            o_ref[...] += _dot(y, wout_ref[...]).reshape(nb, tt, d)
        if last:
            _post_residual(o_ref, x_ref, gpost_ref, 1.0)

    _chunk_variants(c, nc, body)

    @pl.when(jnp.logical_and(t == nt - 1, c == nc - 1))
    def _():
        for k in range(nc):
            nh_ref[:, :, k * tn:(k + 1) * tn] = carry_ref[k, :, 0:nh, :]


def _sconv_mixer(x, hist, g_pre, g_post, w_in, w_conv, w_out, slot, gidx):
    b, t, d = x.shape
    nb, tt = _seq_tiles(b, t)
    width = w_conv.shape[1]
    nh = width - 1
    assert tt >= nh
    tn = min(COL_CHUNK, d)
    nc = d // tn
    hp = _hist_pad(nh)
    blocks = _nbytes((nb, tt, d), F32) * 2 + _nbytes((d, tn), BF16) * 4
    scratch = (_nbytes((nb * tt, d), BF16) + _nbytes((nc, nb, SUBLANES, tn), F32)
               + _nbytes((nb, hp + tt, tn), F32))
    temps = _nbytes((nb * tt, tn), F32) * 6 + _nbytes((nb * tt, d), F32)
    return pl.pallas_call(
        functools.partial(_sconv_kernel, nc=nc),
        grid=(b // nb, t // tt, nc),
        in_specs=[
            pl.BlockSpec((nb, tt, d), lambda i, j, c: (i, j, 0)),
            pl.BlockSpec((nb, nh, tn), lambda i, j, c: (i, 0, c)),
            pl.BlockSpec((None, 1, d), lambda i, j, c: (gidx, 0, 0)),
            pl.BlockSpec((None, 1, d), lambda i, j, c: (gidx, 0, 0)),
            pl.BlockSpec((None, d, tn), lambda i, j, c: (slot, 0, c)),
            pl.BlockSpec((None, d, tn), lambda i, j, c: (slot, 0, nc + c)),
            pl.BlockSpec((None, d, tn), lambda i, j, c: (slot, 0, 2 * nc + c)),
            pl.BlockSpec((None, width, tn), lambda i, j, c: (slot, 0, c)),
            pl.BlockSpec((None, tn, d), lambda i, j, c: (slot, c, 0)),
        ],
        out_specs=[
            pl.BlockSpec((nb, tt, d), lambda i, j, c: (i, j, 0)),
            pl.BlockSpec((nb, nh, d), lambda i, j, c: (i, 0, 0)),
        ],
        out_shape=[jax.ShapeDtypeStruct((b, t, d), F32),
                   jax.ShapeDtypeStruct((b, nh, d), F32)],
        scratch_shapes=[pltpu.VMEM((nb * tt, d), BF16),
                        pltpu.VMEM((nc, nb, SUBLANES, tn), F32),
                        pltpu.VMEM((nb, hp + tt, tn), F32)],
        compiler_params=pltpu.CompilerParams(
            dimension_semantics=("parallel", "arbitrary", "arbitrary"),
            vmem_limit_bytes=_vmem_limit(blocks, scratch, temps)),
        name="sconv_mixer",
    )(x, hist, g_pre, g_post, w_in, w_in, w_in, w_conv, w_out)


def _cconv_kernel(x_ref, hist_ref, gpre_ref, gpost_ref, wa_ref, wg_ref, ba_ref, bg_ref,
                  wdw_ref, bdw_ref, lng_ref, lnb_ref, w2_ref, b2_ref,
                  o_ref, nh_ref, u_ref, hc_ref, carry_ref, ext_ref, *, nc):
    t = pl.program_id(1)
    nt = pl.num_programs(1)
    c = pl.program_id(2)
    nb, tt, d = x_ref.shape
    tn = wa_ref.shape[-1]
    width = wdw_ref.shape[0]
    nh = width - 1
    hp = _hist_pad(nh)
    m = nb * tt

    @pl.when(jnp.logical_and(c < nc, t == 0))
    def _():
        ext_ref[:, 0:hp, :] = jnp.zeros((nb, hp, tn), F32)
        ext_ref[:, hp - nh:hp, :] = hist_ref[...]

    @pl.when(jnp.logical_and(c < nc, t > 0))
    def _():
        ext_ref[:, hp - nh:hp, :] = carry_ref[c, :, 0:nh, :]

    def project_and_conv(first):
        if first:
            u_ref[...] = _rms(x_ref[...], gpre_ref[...]).reshape(m, d).astype(BF16)
        u = u_ref[...]
        a = _dot(u, wa_ref[...]) + ba_ref[...]
        gate = _dot(u, wg_ref[...]) + bg_ref[...]
        v = (a * jax.nn.sigmoid(gate)).reshape(nb, tt, tn)
        ext_ref[:, hp:hp + tt, :] = v
        carry_ref[c, :, 0:nh, :] = v[:, tt - nh:tt, :]
        conv = _causal_depthwise(lambda off, rows: ext_ref[:, off:off + rows, :],
                                 wdw_ref[...], tt, hp - nh) + bdw_ref[...]
        hc_ref[c] = conv.reshape(m, tn)

    pl.when(c == 0)(lambda: project_and_conv(True))
    if nc > 1:
        pl.when(jnp.logical_and(c > 0, c < nc))(lambda: project_and_conv(False))

    def layernorm_silu():
        s1 = jnp.zeros((m, 1), F32)
        for k in range(nc):
            s1 = s1 + jnp.sum(hc_ref[k], axis=-1, keepdims=True)
        mean = s1 / d
        s2 = jnp.zeros((m, 1), F32)
        for k in range(nc):
            xc = hc_ref[k] - mean
            s2 = s2 + jnp.sum(xc * xc, axis=-1, keepdims=True)
        rstd = lax.rsqrt(s2 / d + EPS)
        for k in range(nc):
            sl = slice(k * tn, (k + 1) * tn)
            hn = (hc_ref[k] - mean) * rstd * lng_ref[:, sl] + lnb_ref[:, sl]
            u_ref[:, sl] = (hn * jax.nn.sigmoid(hn)).astype(BF16)

    for k in range(nc):
        @pl.when(c == nc + k)
        def _(k=k):
            if k == 0:
                layernorm_silu()
            sl = slice(k * tn, (k + 1) * tn)
            y = _dot(u_ref[...], w2_ref[...]) + b2_ref[:, sl]
            o_ref[:, :, sl] = y.reshape(nb, tt, tn)
            if k == nc - 1:
                _post_residual(o_ref, x_ref, gpost_ref, 1.0)

    @pl.when(jnp.logical_and(t == nt - 1, c == 2 * nc - 1))
    def _():
        for k in range(nc):
            nh_ref[:, :, k * tn:(k + 1) * tn] = carry_ref[k, :, 0:nh, :]


def _cconv_mixer(x, hist, g_pre, g_post, w_pw1, b_pw1, w_dw, b_dw, ln_g, ln_b,
                 w_pw2, b_pw2, slot, gidx):
    b, t, d = x.shape
    nb, tt = _seq_tiles(b, t)
    width = w_dw.shape[1]
    nh = width - 1
    assert tt >= nh
    tn = min(COL_CHUNK, d)
    nc = d // tn
    hp = _hist_pad(nh)
    m = nb * tt

    def p1(c):
        return jnp.minimum(c, nc - 1)

    def p2(c):
        return jnp.maximum(c - nc, 0)

    blocks = (_nbytes((nb, tt, d), F32) * 2 + _nbytes((d, tn), BF16) * 3
              + _nbytes((nb, nh, tn), F32) * 2)
    scratch = (_nbytes((m, d), BF16) + _nbytes((nc, m, tn), F32)
               + _nbytes((nc, nb, hp, tn), F32) + _nbytes((nb, hp + tt, tn), F32))
    temps = _nbytes((m, tn), F32) * 6 + _nbytes((m, d), F32)
    return pl.pallas_call(
        functools.partial(_cconv_kernel, nc=nc),
        grid=(b // nb, t // tt, 2 * nc),
        in_specs=[
            pl.BlockSpec((nb, tt, d), lambda i, j, c: (i, j, 0)),
            pl.BlockSpec((nb, nh, tn), lambda i, j, c: (i, 0, p1(c))),
            pl.BlockSpec((None, 1, d), lambda i, j, c: (gidx, 0, 0)),
            pl.BlockSpec((None, 1, d), lambda i, j, c: (gidx, 0, 0)),
            pl.BlockSpec((None, d, tn), lambda i, j, c: (slot, 0, p1(c))),
            pl.BlockSpec((None, d, tn), lambda i, j, c: (slot, 0, nc + p1(c))),
            pl.BlockSpec((None, 1, tn), lambda i, j, c: (slot, 0, p1(c))),
            pl.BlockSpec((None, 1, tn), lambda i, j, c: (slot, 0, nc + p1(c))),
            pl.BlockSpec((None, width, tn), lambda i, j, c: (slot, 0, p1(c))),
            pl.BlockSpec((None, 1, tn), lambda i, j, c: (slot, 0, p1(c))),
            pl.BlockSpec((None, 1, d), lambda i, j, c: (slot, 0, 0)),
            pl.BlockSpec((None, 1, d), lambda i, j, c: (slot, 0, 0)),
            pl.BlockSpec((None, d, tn), lambda i, j, c: (slot, 0, p2(c))),
            pl.BlockSpec((None, 1, d), lambda i, j, c: (slot, 0, 0)),
        ],
        out_specs=[
            pl.BlockSpec((nb, tt, d), lambda i, j, c: (i, j, 0)),
            pl.BlockSpec((nb, nh, d), lambda i, j, c: (i, 0, 0)),
        ],
        out_shape=[jax.ShapeDtypeStruct((b, t, d), F32),
                   jax.ShapeDtypeStruct((b, nh, d), F32)],
        scratch_shapes=[pltpu.VMEM((m, d), BF16),
                        pltpu.VMEM((nc, m, tn), F32),
                        pltpu.VMEM((nc, nb, hp, tn), F32),
                        pltpu.VMEM((nb, hp + tt, tn), F32)],
        compiler_params=pltpu.CompilerParams(
            dimension_semantics=("parallel", "arbitrary", "arbitrary"),
            vmem_limit_bytes=_vmem_limit(blocks, scratch, temps)),
        name="cconv_mixer",
    )(x, hist, g_pre, g_post, w_pw1, w_pw1, b_pw1, b_pw1, w_dw, b_dw, ln_g, ln_b,
      w_pw2, b_pw2)


def _xattn_kernel(x_ref, k_ref, v_ref, gpre_ref, gpost_ref, wq_ref, wo_ref,
                  o_ref, u_ref, *, nsteps, hd, tt):
    step = pl.program_id(2)
    m, d = x_ref.shape
    nb = k_ref.shape[0]
    heads_here = wq_ref.shape[-1] // hd
    sub = min(m, XATTN_SUB_ROWS)
    seq_rows = min(sub, tt)

    def body(first, last):
        for r in range(m // sub):
            rows = slice(r * sub, (r + 1) * sub)
            if first:
                u_ref[rows, :] = _rms(x_ref[rows, :], gpre_ref[...]).astype(BF16)
            q = _dot(u_ref[rows, :], wq_ref[...]).astype(BF16)
            outs = []
            for s0 in range(0, sub, seq_rows):
                b = (r * sub + s0) // tt if nb > 1 else 0
                per_head = []
                for hh in range(heads_here):
                    cols = slice(hh * hd, (hh + 1) * hd)
                    kb = k_ref[b, :, cols].astype(BF16)
                    vb = v_ref[b, :, cols].astype(BF16)
                    s = lax.dot_general(q[s0:s0 + seq_rows, cols], kb, (((1,), (1,)), ((), ())),
                                        preferred_element_type=F32) * (hd ** -0.5)
                    e = jnp.exp(s - jnp.max(s, axis=-1, keepdims=True))
                    p = (e / jnp.sum(e, axis=-1, keepdims=True)).astype(BF16)
                    per_head.append(_dot(p, vb).astype(BF16))
                outs.append(per_head[0] if heads_here == 1 else jnp.concatenate(per_head, axis=1))
            o = outs[0] if len(outs) == 1 else jnp.concatenate(outs, axis=0)
            if first:
                o_ref[rows, :] = _dot(o, wo_ref[...])
            else:
                o_ref[rows, :] += _dot(o, wo_ref[...])
            if last:
                _post_residual(o_ref, x_ref, gpost_ref, 1.0, (rows, slice(None)))

    _chunk_variants(step, nsteps, body)


def _xattn(x, mem_k, mem_v, g_pre, g_post, w_q, w_o, nheads, layer, gidx):
    b, t, d = x.shape
    nb, tt = _seq_tiles(b, t, XATTN_ROW_TILE if t >= XATTN_ROW_TILE else XATTN_SHORT_ROWS)
    n_mem = mem_k.shape[2]
    hd = d // nheads
    hs = XATTN_HEADS_PER_STEP if nheads % XATTN_HEADS_PER_STEP == 0 else 1
    cw = hs * hd
    nsteps = nheads // hs
    m = nb * tt
    nt = t // tt
    sub = min(m, XATTN_SUB_ROWS)
    assert m % sub == 0 and (sub % tt == 0 or tt % sub == 0)
    blocks = (_nbytes((m, d), F32) * 2 + _nbytes((nb, n_mem, cw), mem_k.dtype) * 2
              + _nbytes((d, cw), BF16) * 2)
    scratch = _nbytes((m, d), BF16)
    temps = _nbytes((sub, cw), F32) * 4 + _nbytes((sub, n_mem), F32) * 4 * hs
    out = pl.pallas_call(
        functools.partial(_xattn_kernel, nsteps=nsteps, hd=hd, tt=tt),
        grid=(b // nb, nt, nsteps),
        in_specs=[
            pl.BlockSpec((m, d), lambda i, j, h: (i * nt + j, 0)),
            pl.BlockSpec((None, nb, n_mem, cw), lambda i, j, h: (layer, i, 0, h)),
            pl.BlockSpec((None, nb, n_mem, cw), lambda i, j, h: (layer, i, 0, h)),
            pl.BlockSpec((None, 1, d), lambda i, j, h: (gidx, 0, 0)),
            pl.BlockSpec((None, 1, d), lambda i, j, h: (gidx, 0, 0)),
            pl.BlockSpec((None, d, cw), lambda i, j, h: (layer, 0, h)),
            pl.BlockSpec((None, cw, d), lambda i, j, h: (layer, h, 0)),
        ],
        out_specs=pl.BlockSpec((m, d), lambda i, j, h: (i * nt + j, 0)),
        out_shape=jax.ShapeDtypeStruct((b * t, d), F32),
        scratch_shapes=[pltpu.VMEM((m, d), BF16)],
        compiler_params=pltpu.CompilerParams(
            dimension_semantics=("parallel", "arbitrary", "arbitrary"),
            vmem_limit_bytes=_vmem_limit(blocks, scratch, temps)),
        name="xattn",
    )(x.reshape(b * t, d), mem_k, mem_v, g_pre, g_post, w_q, w_o)
    return out.reshape(b, t, d)


def _memkv_kernel(mem_ref, g_ref, wk_ref, wv_ref, k_ref, v_ref, kb_ref, vb_ref, m_ref,
                  *, nheads):
    c = pl.program_id(2)

    @pl.when(c == 0)
    def _():
        m_ref[...] = _rms(mem_ref[...], g_ref[...]).astype(BF16)

    mm = m_ref[...]
    k = _dot(mm, wk_ref[...])
    v = _dot(mm, wv_ref[...])
    kb_ref[...] = k.astype(BF16)
    vb_ref[...] = v.astype(BF16)
    for hh in range(nheads):
        @pl.when(c == hh)
        def _(hh=hh):
            k_ref[:, hh, :] = k
            v_ref[:, hh, :] = v


def _mem_kv(mem, g_mem, w_k, w_v, nheads):
    n, d = mem.shape
    depth = w_k.shape[0]
    hd = d // nheads
    tm = min(ROW_TILE, n)
    assert n % tm == 0
    padded_heads = -(-nheads // SUBLANES) * SUBLANES
    blocks = (_nbytes((tm, d), F32) + _nbytes((d, hd), BF16) * 2
              + _nbytes((tm, padded_heads, hd), F32) * 2 + _nbytes((tm, hd), BF16) * 2)
    scratch = _nbytes((tm, d), BF16)
    temps = _nbytes((tm, hd), F32) * 4
    out = jax.ShapeDtypeStruct((depth, n, nheads, hd), F32)
    out_bf = jax.ShapeDtypeStruct((depth, n, d), BF16)
    return pl.pallas_call(
        functools.partial(_memkv_kernel, nheads=nheads),
        grid=(depth, n // tm, nheads),
        in_specs=[
            pl.BlockSpec((tm, d), lambda l, i, c: (i, 0)),
            pl.BlockSpec((None, 1, d), lambda l, i, c: (l, 0, 0)),
            pl.BlockSpec((None, d, hd), lambda l, i, c: (l, 0, c)),
            pl.BlockSpec((None, d, hd), lambda l, i, c: (l, 0, c)),
        ],
        out_specs=[pl.BlockSpec((None, tm, nheads, hd), lambda l, i, c: (l, i, 0, 0)),
                   pl.BlockSpec((None, tm, nheads, hd), lambda l, i, c: (l, i, 0, 0)),
                   pl.BlockSpec((None, tm, hd), lambda l, i, c: (l, i, c)),
                   pl.BlockSpec((None, tm, hd), lambda l, i, c: (l, i, c))],
        out_shape=[out, out, out_bf, out_bf],
        scratch_shapes=[pltpu.VMEM((tm, d), BF16)],
        compiler_params=pltpu.CompilerParams(
            dimension_semantics=("parallel", "parallel", "arbitrary"),
            vmem_limit_bytes=_vmem_limit(blocks, scratch, temps)),
        name="mem_kv",
    )(mem, g_mem, w_k, w_v)


def kernel(x_prompt, x_sample, mem_prompt, cache_pool, cache_sconv, cache_cconv, cache_mem_k, cache_mem_v, g_pre, g_post, g_mem, ffn_w_gu, ffn_w_down, xa_w_q, xa_w_k, xa_w_v, xa_w_o, pool_w, pool_scale, sc_w_in, sc_w_conv, sc_w_out, cc_w_pw1, cc_b_pw1, cc_w_dw, cc_b_dw, cc_ln_g, cc_ln_b, cc_w_pw2, cc_b_pw2):
    depth = g_pre.shape[0]
    d = x_prompt.shape[-1]
    bp = x_prompt.shape[0]
    n_mem = mem_prompt.shape[1]
    heads, head_dim = cache_mem_k.shape[3], cache_mem_k.shape[4]

    w_gu, w_down = ffn_w_gu.astype(BF16), ffn_w_down.astype(BF16)
    w_q, w_k, w_v, w_o = (w.astype(BF16) for w in (xa_w_q, xa_w_k, xa_w_v, xa_w_o))
    w_pool = pool_w.astype(BF16)
    w_sc_in, w_sc_out = sc_w_in.astype(BF16), sc_w_out.astype(BF16)
    w_pw1, w_pw2 = cc_w_pw1.astype(BF16), cc_w_pw2.astype(BF16)

    def rows(a):
        return a.reshape(-1, 1, a.shape[-1])

    gp, gq = rows(g_pre), rows(g_post)
    gm = rows(g_mem)
    p_scale = rows(pool_scale)
    b_pw1, b_dw, ln_g, ln_b, b_pw2 = (rows(a) for a in (cc_b_pw1, cc_b_dw, cc_ln_g, cc_ln_b, cc_b_pw2))

    mk, mv, mk_bf, mv_bf = _mem_kv(mem_prompt.reshape(bp * n_mem, d), gm, w_k, w_v, heads)
    mem_k_p = mk_bf.reshape(depth, bp, n_mem, d)
    mem_v_p = mv_bf.reshape(depth, bp, n_mem, d)
    mem_k_s = cache_mem_k.reshape(depth, -1, n_mem, d).astype(BF16)
    mem_v_s = cache_mem_v.reshape(depth, -1, n_mem, d).astype(BF16)

    def trunk(x, pos0, pool_hist, sconv_hist, cconv_hist, mem_k, mem_v):
        b, t, _ = x.shape
        new_pool, new_sconv, new_cconv = [], [], []
        for i in range(depth):
            x = _ffn(x.reshape(b * t, d), gp, gq, w_gu, w_down, i, 0, 4 * i).reshape(b, t, d)
            kind, slot = i % N_MIXERS, i // N_MIXERS
            if kind == 0:
                x, st = _pool_mixer(x, pool_hist[slot], gp, gq, w_pool, p_scale, slot, 4 * i + 1, pos0)
                new_pool.append(st)
            elif kind == 1:
                x, st = _sconv_mixer(x, sconv_hist[slot], gp, gq, w_sc_in, sc_w_conv, w_sc_out,
                                     slot, 4 * i + 1)
                new_sconv.append(st)
            else:
                x, st = _cconv_mixer(x, cconv_hist[slot], gp, gq, w_pw1, b_pw1, cc_w_dw, b_dw,
                                     ln_g, ln_b, w_pw2, b_pw2, slot, 4 * i + 1)
                new_cconv.append(st)
            x = _xattn(x, mem_k, mem_v, gp, gq, w_q, w_o, heads, i, 4 * i + 2)
            x = _ffn(x.reshape(b * t, d), gp, gq, w_gu, w_down, i, 1, 4 * i + 3).reshape(b, t, d)
        return x, jnp.stack(new_pool), jnp.stack(new_sconv), jnp.stack(new_cconv)

    dt = x_prompt.dtype
    n_pool, n_sconv, n_cconv = cache_pool.shape[0], cache_sconv.shape[0], cache_cconv.shape[0]
    pool_h0 = [jnp.zeros((bp, POOL_HIST, d), dt)] * n_pool
    sconv_h0 = [jnp.zeros((bp, cache_sconv.shape[2], d), dt)] * n_sconv
    cconv_h0 = [jnp.zeros((bp, cache_cconv.shape[2], d), dt)] * n_cconv
    y_prompt, pool_p, sconv_p, cconv_p = trunk(x_prompt, 0, pool_h0, sconv_h0, cconv_h0, mem_k_p, mem_v_p)
    y_sample, pool_s, sconv_s, cconv_s = trunk(
        x_sample, PAST_LEN,
        [cache_pool[j] for j in range(n_pool)],
        [cache_sconv[j] for j in range(n_sconv)],
        [cache_cconv[j] for j in range(n_cconv)],
        mem_k_s, mem_v_s)

    mem_k_prompt = mk.reshape(depth, bp, n_mem, heads, head_dim)
    mem_v_prompt = mv.reshape(depth, bp, n_mem, heads, head_dim)
    return (y_prompt, y_sample, pool_p, pool_s, sconv_p, sconv_s, cconv_p, cconv_s,
            mem_k_prompt, mem_v_prompt)
```

```python
import functools

import jax
import jax.numpy as jnp
from jax import lax
from jax.experimental import pallas as pl
from jax.experimental.pallas import tpu as pltpu

EPS = 1e-6
MACARON = 0.5
PAST_LEN = 1024
POOL_WINDOWS = (2, 4, 8, 16)
POOL_HIST = max(POOL_WINDOWS) - 1
N_MIXERS = 3

BF16 = jnp.bfloat16
F32 = jnp.float32

V7X_VMEM_BYTES = 64 * 1024 * 1024
SUBLANES = 8

assert all(w & (w - 1) == 0 for w in POOL_WINDOWS)
POOL_PAD = SUBLANES * (max(POOL_WINDOWS).bit_length() - 1)
assert POOL_PAD >= POOL_HIST + 1

ROW_TILE = 512
FFN_ROW_TILE = 1024
FFN_SUB_ROWS = 512
CONV_ROW_BLOCK = 64
XATTN_ROW_TILE = 1024
XATTN_SUB_ROWS = 512
XATTN_HEADS_PER_STEP = 2
XATTN_SHORT_ROWS = 128
COL_CHUNK = 512


def _vmem_limit(block_bytes, scratch_bytes, temp_bytes):
    est = 2 * block_bytes + scratch_bytes + temp_bytes
    return int(min(est + est // 8, V7X_VMEM_BYTES - 6 * 1024 * 1024))


def _nbytes(shape, dtype):
    n = 1
    for s in shape:
        n *= s
    return n * jnp.dtype(dtype).itemsize


def _rms(x, g):
    ms = jnp.mean(x * x, axis=-1, keepdims=True)
    return x * lax.rsqrt(ms + EPS) * g


def _dot(a, b):
    return jnp.dot(a, b, preferred_element_type=F32)


def _hist_pad(h):
    return -(-h // SUBLANES) * SUBLANES


def _post_residual(o_ref, x_ref, g_ref, scale, idx=Ellipsis):
    o = o_ref[idx]
    ms = jnp.mean(o * o, axis=-1, keepdims=True)
    o_ref[idx] = x_ref[idx] + o_ref[idx] * (scale * lax.rsqrt(ms + EPS)) * g_ref[...]


def _causal_depthwise(load, w, tt, start0):
    width = w.shape[0]
    out = None
    for s in range(SUBLANES):
        taps = [k for k in range(width) if (start0 + k) % SUBLANES == s]
        if not taps:
            continue
        rows = tt if s == 0 else tt + SUBLANES
        part = None
        for k in taps:
            off = start0 + k - s
            term = w[k:k + 1, :] * load(off, rows)
            part = term if part is None else part + term
        if s:
            part = part[:, s:s + tt, :]
        out = part if out is None else out + part
    return out


def _row_blocks(nb, tt, rb):
    if tt >= rb:
        assert tt % rb == 0
        return [(slice(b, b + 1), t0, rb) for b in range(nb) for t0 in range(0, tt, rb)]
    per = max(1, rb // tt)
    assert nb % per == 0
    return [(slice(b, b + per), 0, tt) for b in range(0, nb, per)]


def _chunk_variants(c, nc, body):
    if nc == 1:
        body(True, True)
        return
    pl.when(c == 0)(lambda: body(True, False))
    if nc > 2:
        pl.when(jnp.logical_and(c > 0, c < nc - 1))(lambda: body(False, False))
    pl.when(c == nc - 1)(lambda: body(False, True))


def _ffn_kernel(x_ref, gpre_ref, gpost_ref, wg_ref, wu_ref, wd_ref, o_ref, h_ref, *, nj):
    j = pl.program_id(1)

    tm, d = x_ref.shape
    sub = min(tm, FFN_SUB_ROWS)

    def body(first, last):
        for r in range(tm // sub):
            rows = slice(r * sub, (r + 1) * sub)
            if first:
                h_ref[rows, :] = _rms(x_ref[rows, :], gpre_ref[...]).astype(BF16)
            h = h_ref[rows, :]
            gate = _dot(h, wg_ref[...])
            up = _dot(h, wu_ref[...])
            a = (gate * jax.nn.sigmoid(gate) * up).astype(BF16)
            if first:
                o_ref[rows, :] = _dot(a, wd_ref[...])
            else:
                o_ref[rows, :] += _dot(a, wd_ref[...])
            if last:
                _post_residual(o_ref, x_ref, gpost_ref, MACARON, (rows, slice(None)))

    _chunk_variants(j, nj, body)


def _ffn(x, g_pre, g_post, w_gu, w_down, layer, slot, gidx):
    n, d = x.shape
    dff = w_down.shape[2]
    tm = min(FFN_ROW_TILE, n)
    tf = min(COL_CHUNK, dff)
    nj = dff // tf
    assert n % tm == 0 and dff % tf == 0
    blocks = (_nbytes((tm, d), F32) * 2 + _nbytes((d, tf), BF16) * 3)
    scratch = _nbytes((tm, d), BF16)
    temps = _nbytes((min(tm, FFN_SUB_ROWS), tf), F32) * 6
    return pl.pallas_call(
        functools.partial(_ffn_kernel, nj=nj),
        grid=(n // tm, nj),
        in_specs=[
            pl.BlockSpec((tm, d), lambda i, j: (i, 0)),
            pl.BlockSpec((None, 1, d), lambda i, j: (gidx, 0, 0)),
            pl.BlockSpec((None, 1, d), lambda i, j: (gidx, 0, 0)),
            pl.BlockSpec((None, None, d, tf), lambda i, j: (layer, slot, 0, j)),
            pl.BlockSpec((None, None, d, tf), lambda i, j: (layer, slot, 0, nj + j)),
            pl.BlockSpec((None, None, tf, d), lambda i, j: (layer, slot, j, 0)),
        ],
        out_specs=pl.BlockSpec((tm, d), lambda i, j: (i, 0)),
        out_shape=jax.ShapeDtypeStruct((n, d), F32),
        scratch_shapes=[pltpu.VMEM((tm, d), BF16)],
        compiler_params=pltpu.CompilerParams(
            dimension_semantics=("parallel", "arbitrary"),
            vmem_limit_bytes=_vmem_limit(blocks, scratch, temps)),
        name="ffn",
    )(x, g_pre, g_post, w_gu, w_gu, w_down)


def _seq_tiles(b, t, rows=None):
    rows = ROW_TILE if rows is None else rows
    if t >= rows:
        assert t % rows == 0
        return 1, rows
    nb = min(b, max(1, rows // t))
    assert b % nb == 0 and t % SUBLANES == 0
    return nb, t


def _pool_kernel(x_ref, hist_ref, gpre_ref, gpost_ref, w_ref, scale_ref,
                 o_ref, nh_ref, ext_ref, *, pos0):
    t = pl.program_id(1)
    nt = pl.num_programs(1)
    nb, tt, d = x_ref.shape
    hp = POOL_PAD
    gc = d // len(POOL_WINDOWS)

    @pl.when(t == 0)
    def _():
        ext_ref[:, 0:hp, :] = jnp.zeros((nb, hp, d), F32)
        ext_ref[:, hp - POOL_HIST:hp, :] = hist_ref[...]

    @pl.when(t > 0)
    def _():
        ext_ref[:, hp - POOL_HIST:hp, :] = ext_ref[:, hp + tt - POOL_HIST:hp + tt, :]

    x = x_ref[...]
    ext_ref[:, hp:hp + tt, :] = _rms(x, gpre_ref[...])
    pos = pos0 + t * tt + lax.broadcasted_iota(jnp.int32, (1, tt, 1), 1)
    span_rows = tt + hp - SUBLANES
    ys = []
    for g, w in enumerate(POOL_WINDOWS):
        sl = slice(g * gc, (g + 1) * gc)
        u = ext_ref[:, hp:hp + tt, sl]
        win = (ext_ref[:, SUBLANES:SUBLANES + span_rows, sl]
               + ext_ref[:, SUBLANES - 1:SUBLANES - 1 + span_rows, sl])
        span, origin = 2, 0
        while span < w:
            n = win.shape[1]
            win = win[:, SUBLANES:, :] + win[:, SUBLANES - span:n - span, :]
            span, origin = 2 * span, origin + SUBLANES
        first = hp - SUBLANES - origin
        win = win[:, first:first + tt, :]
        inv_cnt = 1.0 / jnp.minimum(pos + 1, w).astype(F32)
        dg = (win * inv_cnt - u).reshape(nb * tt, gc).astype(BF16)
        ys.append(_dot(dg, w_ref[g]))
    y = jnp.concatenate(ys, axis=-1) * scale_ref[...]
    out = x.reshape(nb * tt, d) + _rms(y, gpost_ref[...])
    o_ref[...] = out.reshape(nb, tt, d)

    @pl.when(t == nt - 1)
    def _():
        nh_ref[...] = ext_ref[:, hp + tt - POOL_HIST:hp + tt, :]


def _pool_mixer(x, hist, g_pre, g_post, pool_w, pool_scale, slot, gidx, pos0):
    b, t, d = x.shape
    nb, tt = _seq_tiles(b, t)
    assert tt >= POOL_HIST
    ng = len(POOL_WINDOWS)
    gc = d // ng
    hp = POOL_PAD
    blocks = _nbytes((nb, tt, d), F32) * 2 + _nbytes((ng, gc, gc), BF16)
    scratch = _nbytes((nb, hp + tt, d), F32)
    temps = _nbytes((nb, tt, d), F32) * 4
    return pl.pallas_call(
        functools.partial(_pool_kernel, pos0=pos0),
        grid=(b // nb, t // tt),
        in_specs=[
            pl.BlockSpec((nb, tt, d), lambda i, j: (i, j, 0)),
            pl.BlockSpec((nb, POOL_HIST, d), lambda i, j: (i, 0, 0)),
            pl.BlockSpec((None, 1, d), lambda i, j: (gidx, 0, 0)),
            pl.BlockSpec((None, 1, d), lambda i, j: (gidx, 0, 0)),
            pl.BlockSpec((None, ng, gc, gc), lambda i, j: (slot, 0, 0, 0)),
            pl.BlockSpec((None, 1, d), lambda i, j: (slot, 0, 0)),
        ],
        out_specs=[
            pl.BlockSpec((nb, tt, d), lambda i, j: (i, j, 0)),
            pl.BlockSpec((nb, POOL_HIST, d), lambda i, j: (i, 0, 0)),
        ],
        out_shape=[jax.ShapeDtypeStruct((b, t, d), F32),
                   jax.ShapeDtypeStruct((b, POOL_HIST, d), F32)],
        scratch_shapes=[pltpu.VMEM((nb, hp + tt, d), F32)],
        compiler_params=pltpu.CompilerParams(
            dimension_semantics=("parallel", "arbitrary"),
            vmem_limit_bytes=_vmem_limit(blocks, scratch, temps)),
        name="pool_mixer",
    )(x, hist, g_pre, g_post, pool_w, pool_scale)


def _sconv_kernel(x_ref, hist_ref, gpre_ref, gpost_ref, wb_ref, wc_ref, wh_ref,
                  wconv_ref, wout_ref, o_ref, nh_ref, u_ref, carry_ref, ext_ref, *, nc):
    t = pl.program_id(1)
    nt = pl.num_programs(1)
    c = pl.program_id(2)
    nb, tt, d = x_ref.shape
    tn = wb_ref.shape[-1]
    width = wconv_ref.shape[0]
    nh = width - 1
    hp = _hist_pad(nh)

    @pl.when(t == 0)
    def _():
        ext_ref[:, 0:hp, :] = jnp.zeros((nb, hp, tn), F32)
        ext_ref[:, hp - nh:hp, :] = hist_ref[...]

    @pl.when(t > 0)
    def _():
        ext_ref[:, hp - nh:hp, :] = carry_ref[c, :, 0:nh, :]

    def body(first, last):
        if first:
            u_ref[...] = _rms(x_ref[...], gpre_ref[...]).reshape(nb * tt, d).astype(BF16)
        u = u_ref[...]
        gb = _dot(u, wb_ref[...])
        v = (_dot(u, wc_ref[...]) * _dot(u, wh_ref[...])).reshape(nb, tt, tn)
        ext_ref[:, hp:hp + tt, :] = v
        carry_ref[c, :, 0:nh, :] = v[:, tt - nh:tt, :]
        conv = _causal_depthwise(lambda off, rows: ext_ref[:, off:off + rows, :],
                                 wconv_ref[...], tt, hp - nh)
        y =(gb.reshape(nb, tt, tn) * conv).reshape(nb * tt, tn).astype(BF16)
        if first:
            o_ref[...] = _dot(y, wout_ref[...]).reshape(nb, tt, d)
        else:
            o_ref[...] += _dot(y, wout_ref[...]).reshape(nb, tt, d)
        if last:
            _post_residual(o_ref, x_ref, gpost_ref, 1.0)

    _chunk_variants(c, nc, body)

    @pl.when(jnp.logical_and(t == nt - 1, c == nc - 1))
    def _():
        for k in range(nc):
            nh_ref[:, :, k * tn:(k + 1) * tn] = carry_ref[k, :, 0:nh, :]


def _sconv_mixer(x, hist, g_pre, g_post, w_in, w_conv, w_out, slot, gidx):
    b, t, d = x.shape
    nb, tt = _seq_tiles(b, t)
    width = w_conv.shape[1]
    nh = width - 1
    assert tt >= nh
    tn = min(COL_CHUNK, d)
    nc = d // tn
    hp = _hist_pad(nh)
    blocks = _nbytes((nb, tt, d), F32) * 2 + _nbytes((d, tn), BF16) * 4
    scratch = (_nbytes((nb * tt, d), BF16) + _nbytes((nc, nb, SUBLANES, tn), F32)
               + _nbytes((nb, hp + tt, tn), F32))
    temps = _nbytes((nb * tt, tn), F32) * 6 + _nbytes((nb * tt, d), F32)
    return pl.pallas_call(
        functools.partial(_sconv_kernel, nc=nc),
        grid=(b // nb, t // tt, nc),
        in_specs=[
            pl.BlockSpec((nb, tt, d), lambda i, j, c: (i, j, 0)),
            pl.BlockSpec((nb, nh, tn), lambda i, j, c: (i, 0, c)),
            pl.BlockSpec((None, 1, d), lambda i, j, c: (gidx, 0, 0)),
            pl.BlockSpec((None, 1, d), lambda i, j, c: (gidx, 0, 0)),
            pl.BlockSpec((None, d, tn), lambda i, j, c: (slot, 0, c)),
            pl.BlockSpec((None, d, tn), lambda i, j, c: (slot, 0, nc + c)),
            pl.BlockSpec((None, d, tn), lambda i, j, c: (slot, 0, 2 * nc + c)),
            pl.BlockSpec((None, width, tn), lambda i, j, c: (slot, 0, c)),
            pl.BlockSpec((None, tn, d), lambda i, j, c: (slot, c, 0)),
        ],
        out_specs=[
            pl.BlockSpec((nb, tt, d), lambda i, j, c: (i, j, 0)),
            pl.BlockSpec((nb, nh, d), lambda i, j, c: (i, 0, 0)),
        ],
        out_shape=[jax.ShapeDtypeStruct((b, t, d), F32),
                   jax.ShapeDtypeStruct((b, nh, d), F32)],
        scratch_shapes=[pltpu.VMEM((nb * tt, d), BF16),
                        pltpu.VMEM((nc, nb, SUBLANES, tn), F32),
                        pltpu.VMEM((nb, hp + tt, tn), F32)],
        compiler_params=pltpu.CompilerParams(
            dimension_semantics=("parallel", "arbitrary", "arbitrary"),
            vmem_limit_bytes=_vmem_limit(blocks, scratch, temps)),
        name="sconv_mixer",
    )(x, hist, g_pre, g_post, w_in, w_in, w_in, w_conv, w_out)


def _cconv_kernel(x_ref, hist_ref, gpre_ref, gpost_ref, wa_ref, wg_ref, ba_ref, bg_ref,
                  wdw_ref, bdw_ref, lng_ref, lnb_ref, w2_ref, b2_ref,
                  o_ref, nh_ref, u_ref, hc_ref, ext_ref, *, nc):
    t = pl.program_id(1)
    nt = pl.num_programs(1)
    c = pl.program_id(2)
    nb, tt, d = x_ref.shape
    tn = wa_ref.shape[-1]
    width = wdw_ref.shape[0]
    nh = width - 1
    hp = _hist_pad(nh)
    m = nb * tt

    def history(k):
        @pl.when(t == 0)
        def _():
            ext_ref[k, :, 0:hp, :] = jnp.zeros((nb, hp, tn), F32)
            ext_ref[k, :, hp - nh:hp, :] = hist_ref[...]

        @pl.when(t > 0)
        def _():
            ext_ref[k, :, hp - nh:hp, :] = ext_ref[k, :, hp + tt - nh:hp + tt, :]

    def project(k):
        if k == 0:
            u_ref[...] = _rms(x_ref[...], gpre_ref[...]).reshape(m, d).astype(BF16)
        u = u_ref[...]
        a = _dot(u, wa_ref[...]) + ba_ref[...]
        gate = _dot(u, wg_ref[...]) + bg_ref[...]
        ext_ref[k, :, hp:hp + tt, :] = (a * jax.nn.sigmoid(gate)).reshape(nb, tt, tn)

    def conv(k):
        cols = slice(k * tn, (k + 1) * tn)
        wdw = wdw_ref[:, cols]
        for bsl, t0, n in _row_blocks(nb, tt, CONV_ROW_BLOCK):
            blk = _causal_depthwise(lambda off, rows: ext_ref[k, bsl, off:off + rows, :],
                                    wdw, n, hp - nh + t0) + bdw_ref[:, cols]
            r0 = bsl.start * tt + t0
            hc_ref[k, r0:r0 + (bsl.stop - bsl.start) * n, :] = blk.reshape(-1, tn)

    for k in range(nc):
        @pl.when(c == k)
        def _(k=k):
            history(k)
            project(k)
            conv(k)

    def layernorm_silu():
        s1 = jnp.zeros((m, 1), F32)
        for k in range(nc):
            s1 = s1 + jnp.sum(hc_ref[k], axis=-1, keepdims=True)
        mean = s1 / d
        s2 = jnp.zeros((m, 1), F32)
        for k in range(nc):
            xc = hc_ref[k] - mean
            s2 = s2 + jnp.sum(xc * xc, axis=-1, keepdims=True)
        rstd = lax.rsqrt(s2 / d + EPS)
        for k in range(nc):
            sl = slice(k * tn, (k + 1) * tn)
            hn = (hc_ref[k] - mean) * rstd * lng_ref[:, sl] + lnb_ref[:, sl]
            u_ref[:, sl] = (hn * jax.nn.sigmoid(hn)).astype(BF16)

    for k in range(nc):
        @pl.when(c == nc + k)
        def _(k=k):
            if k == 0:
                layernorm_silu()
            sl = slice(k * tn, (k + 1) * tn)
            y = _dot(u_ref[...], w2_ref[...]) + b2_ref[:, sl]
            o_ref[:, :, sl] = y.reshape(nb, tt, tn)
            if k == nc - 1:
                _post_residual(o_ref, x_ref, gpost_ref, 1.0)

    @pl.when(jnp.logical_and(t == nt - 1, c == 2 * nc - 1))
    def _():
        for k in range(nc):
            nh_ref[:, :, k * tn:(k + 1) * tn] = ext_ref[k, :, hp + tt - nh:hp + tt, :]


def _cconv_mixer(x, hist, g_pre, g_post, w_pw1, b_pw1, w_dw, b_dw, ln_g, ln_b,
                 w_pw2, b_pw2, slot, gidx):
    b, t, d = x.shape
    nb, tt = _seq_tiles(b, t)
    width = w_dw.shape[1]
    nh = width - 1
    assert tt >= nh
    tn = min(COL_CHUNK, d)
    nc = d // tn
    hp = _hist_pad(nh)
    m = nb * tt

    def p1(c):
        return jnp.minimum(c, nc - 1)

    def p2(c):
        return jnp.maximum(c - nc, 0)

    blocks = (_nbytes((nb, tt, d), F32) * 2 + _nbytes((d, tn), BF16) * 3
              + _nbytes((nb, nh, tn), F32) * 2)
    scratch = (_nbytes((m, d), BF16) + _nbytes((nc, m, tn), F32)
               + _nbytes((nc, nb, hp + tt, tn), F32))
    temps = _nbytes((m, tn), F32) * 6 + _nbytes((m, d), F32)
    return pl.pallas_call(
        functools.partial(_cconv_kernel, nc=nc),
        grid=(b // nb, t // tt, 2 * nc),
        in_specs=[
            pl.BlockSpec((nb, tt, d), lambda i, j, c: (i, j, 0)),
            pl.BlockSpec((nb, nh, tn), lambda i, j, c: (i, 0, p1(c))),
            pl.BlockSpec((None, 1, d), lambda i, j, c: (gidx, 0, 0)),
            pl.BlockSpec((None, 1, d), lambda i, j, c: (gidx, 0, 0)),
            pl.BlockSpec((None, d, tn), lambda i, j, c: (slot, 0, p1(c))),
            pl.BlockSpec((None, d, tn), lambda i, j, c: (slot, 0, nc + p1(c))),
            pl.BlockSpec((None, 1, tn), lambda i, j, c: (slot, 0, p1(c))),
            pl.BlockSpec((None, 1, tn), lambda i, j, c: (slot, 0, nc + p1(c))),
            pl.BlockSpec((None, width, d), lambda i, j, c: (slot, 0, 0)),
            pl.BlockSpec((None, 1, d), lambda i, j, c: (slot, 0, 0)),
            pl.BlockSpec((None, 1, d), lambda i, j, c: (slot, 0, 0)),
            pl.BlockSpec((None, 1, d), lambda i, j, c: (slot, 0, 0)),
            pl.BlockSpec((None, d, tn), lambda i, j, c: (slot, 0, p2(c))),
            pl.BlockSpec((None, 1, d), lambda i, j, c: (slot, 0, 0)),
        ],
        out_specs=[
            pl.BlockSpec((nb, tt, d), lambda i, j, c: (i, j, 0)),
            pl.BlockSpec((nb, nh, d), lambda i, j, c: (i, 0, 0)),
        ],
        out_shape=[jax.ShapeDtypeStruct((b, t, d), F32),
                   jax.ShapeDtypeStruct((b, nh, d), F32)],
        scratch_shapes=[pltpu.VMEM((m, d), BF16),
                        pltpu.VMEM((nc, m, tn), F32),
                        pltpu.VMEM((nc, nb, hp + tt, tn), F32)],
        compiler_params=pltpu.CompilerParams(
            dimension_semantics=("parallel", "arbitrary", "arbitrary"),
            vmem_limit_bytes=_vmem_limit(blocks, scratch, temps)),
        name="cconv_mixer",
    )(x, hist, g_pre, g_post, w_pw1, w_pw1, b_pw1, b_pw1, w_dw, b_dw, ln_g, ln_b,
      w_pw2, b_pw2)


def _xattn_kernel(x_ref, k_ref, v_ref, gpre_ref, gpost_ref, wq_ref, wo_ref,
                  o_ref, u_ref, *, nsteps, hd, tt):
    step = pl.program_id(2)
    m, d = x_ref.shape
    nb = k_ref.shape[0]
    heads_here = wq_ref.shape[-1] // hd
    sub = min(m, XATTN_SUB_ROWS)
    seq_rows = min(sub, tt)

    def body(first, last):
        for r in range(m // sub):
            rows = slice(r * sub, (r + 1) * sub)
            if first:
                u_ref[rows, :] = _rms(x_ref[rows, :], gpre_ref[...]).astype(BF16)
            q = _dot(u_ref[rows, :], wq_ref[...]).astype(BF16)
            outs = []
            for s0 in range(0, sub, seq_rows):
                b = (r * sub + s0) // tt if nb > 1 else 0
                per_head = []
                for hh in range(heads_here):
                    cols = slice(hh * hd, (hh + 1) * hd)
                    kb = k_ref[b, :, cols].astype(BF16)
                    vb = v_ref[b, :, cols].astype(BF16)
                    s = lax.dot_general(q[s0:s0 + seq_rows, cols], kb, (((1,), (1,)), ((), ())),
                                        preferred_element_type=F32) * (hd ** -0.5)
                    e = jnp.exp(s - jnp.max(s, axis=-1, keepdims=True))
                    p = (e / jnp.sum(e, axis=-1, keepdims=True)).astype(BF16)
                    per_head.append(_dot(p, vb).astype(BF16))
                outs.append(per_head[0] if heads_here == 1 else jnp.concatenate(per_head, axis=1))
            o = outs[0] if len(outs) == 1 else jnp.concatenate(outs, axis=0)
            if first:
                o_ref[rows, :] = _dot(o, wo_ref[...])
            else:
                o_ref[rows, :] += _dot(o, wo_ref[...])
            if last:
                _post_residual(o_ref, x_ref, gpost_ref, 1.0, (rows, slice(None)))

    _chunk_variants(step, nsteps, body)


def _xattn(x, mem_k, mem_v, g_pre, g_post, w_q, w_o, nheads, layer, gidx):
    b, t, d = x.shape
    nb, tt = _seq_tiles(b, t, XATTN_ROW_TILE if t >= XATTN_ROW_TILE else XATTN_SHORT_ROWS)
    n_mem = mem_k.shape[2]
    hd = d // nheads
    hs = XATTN_HEADS_PER_STEP if nheads % XATTN_HEADS_PER_STEP == 0 else 1
    cw = hs * hd
    nsteps = nheads // hs
    m = nb * tt
    nt = t // tt
    sub = min(m, XATTN_SUB_ROWS)
    assert m % sub == 0 and (sub % tt == 0 or tt % sub == 0)
    blocks = (_nbytes((m, d), F32) * 2 + _nbytes((nb, n_mem, cw), mem_k.dtype) * 2
              + _nbytes((d, cw), BF16) * 2)
    scratch = _nbytes((m, d), BF16)
    temps = _nbytes((sub, cw), F32) * 4 + _nbytes((sub, n_mem), F32) * 4 * hs
    out = pl.pallas_call(
        functools.partial(_xattn_kernel, nsteps=nsteps, hd=hd, tt=tt),
        grid=(b // nb, nt, nsteps),
        in_specs=[
            pl.BlockSpec((m, d), lambda i, j, h: (i * nt + j, 0)),
            pl.BlockSpec((None, nb, n_mem, cw), lambda i, j, h: (layer, i, 0, h)),
            pl.BlockSpec((None, nb, n_mem, cw), lambda i, j, h: (layer, i, 0, h)),
            pl.BlockSpec((None, 1, d), lambda i, j, h: (gidx, 0, 0)),
            pl.BlockSpec((None, 1, d), lambda i, j, h: (gidx, 0, 0)),
            pl.BlockSpec((None, d, cw), lambda i, j, h: (layer, 0, h)),
            pl.BlockSpec((None, cw, d), lambda i, j, h: (layer, h, 0)),
        ],
        out_specs=pl.BlockSpec((m, d), lambda i, j, h: (i * nt + j, 0)),
        out_shape=jax.ShapeDtypeStruct((b * t, d), F32),
        scratch_shapes=[pltpu.VMEM((m, d), BF16)],
        compiler_params=pltpu.CompilerParams(
            dimension_semantics=("parallel", "arbitrary", "arbitrary"),
            vmem_limit_bytes=_vmem_limit(blocks, scratch, temps)),
        name="xattn",
    )(x.reshape(b * t, d), mem_k, mem_v, g_pre, g_post, w_q, w_o)
    return out.reshape(b, t, d)


def _memkv_kernel(mem_ref, g_ref, wk_ref, wv_ref, k_ref, v_ref, kb_ref, vb_ref, m_ref,
                  *, nheads):
    c = pl.program_id(2)

    @pl.when(c == 0)
    def _():
        m_ref[...] = _rms(mem_ref[...], g_ref[...]).astype(BF16)

    mm = m_ref[...]
    k = _dot(mm, wk_ref[...])
    v = _dot(mm, wv_ref[...])
    kb_ref[...] = k.astype(BF16)
    vb_ref[...] = v.astype(BF16)
    for hh in range(nheads):
        @pl.when(c == hh)
        def _(hh=hh):
            k_ref[:, hh, :] = k
            v_ref[:, hh, :] = v


def _mem_kv(mem, g_mem, w_k, w_v, nheads):
    n, d = mem.shape
    depth = w_k.shape[0]
    hd = d // nheads
    tm = min(ROW_TILE, n)
    assert n % tm == 0
    padded_heads = -(-nheads // SUBLANES) * SUBLANES
    blocks = (_nbytes((tm, d), F32) + _nbytes((d, hd), BF16) * 2
              + _nbytes((tm, padded_heads, hd), F32) * 2 + _nbytes((tm, hd), BF16) * 2)
    scratch = _nbytes((tm, d), BF16)
    temps = _nbytes((tm, hd), F32) * 4
    out = jax.ShapeDtypeStruct((depth, n, nheads, hd), F32)
    out_bf = jax.ShapeDtypeStruct((depth, n, d), BF16)
    return pl.pallas_call(
        functools.partial(_memkv_kernel, nheads=nheads),
        grid=(depth, n // tm, nheads),
        in_specs=[
            pl.BlockSpec((tm, d), lambda l, i, c: (i, 0)),
            pl.BlockSpec((None, 1, d), lambda l, i, c: (l, 0, 0)),
            pl.BlockSpec((None, d, hd), lambda l, i, c: (l, 0, c)),
            pl.BlockSpec((None, d, hd), lambda l, i, c: (l, 0, c)),
        ],
        out_specs=[pl.BlockSpec((None, tm, nheads, hd), lambda l, i, c: (l, i, 0, 0)),
                   pl.BlockSpec((None, tm, nheads, hd), lambda l, i, c: (l, i, 0, 0)),
                   pl.BlockSpec((None, tm, hd), lambda l, i, c: (l, i, c)),
                   pl.BlockSpec((None, tm, hd), lambda l, i, c: (l, i, c))],
        out_shape=[out, out, out_bf, out_bf],
        scratch_shapes=[pltpu.VMEM((tm, d), BF16)],
        compiler_params=pltpu.CompilerParams(
            dimension_semantics=("parallel", "parallel", "arbitrary"),
            vmem_limit_bytes=_vmem_limit(blocks, scratch, temps)),
        name="mem_kv",
    )(mem, g_mem, w_k, w_v)


def kernel(x_prompt, x_sample, mem_prompt, cache_pool, cache_sconv, cache_cconv, cache_mem_k, cache_mem_v, g_pre, g_post, g_mem, ffn_w_gu, ffn_w_down, xa_w_q, xa_w_k, xa_w_v, xa_w_o, pool_w, pool_scale, sc_w_in, sc_w_conv, sc_w_out, cc_w_pw1, cc_b_pw1, cc_w_dw, cc_b_dw, cc_ln_g, cc_ln_b, cc_w_pw2, cc_b_pw2):
    depth = g_pre.shape[0]
    d = x_prompt.shape[-1]
    bp = x_prompt.shape[0]
    n_mem = mem_prompt.shape[1]
    heads, head_dim = cache_mem_k.shape[3], cache_mem_k.shape[4]

    w_gu, w_down = ffn_w_gu.astype(BF16), ffn_w_down.astype(BF16)
    w_q, w_k, w_v, w_o = (w.astype(BF16) for w in (xa_w_q, xa_w_k, xa_w_v, xa_w_o))
    w_pool = pool_w.astype(BF16)
    w_sc_in, w_sc_out = sc_w_in.astype(BF16), sc_w_out.astype(BF16)
    w_pw1, w_pw2 = cc_w_pw1.astype(BF16), cc_w_pw2.astype(BF16)

    def rows(a):
        return a.reshape(-1, 1, a.shape[-1])

    gp, gq = rows(g_pre), rows(g_post)
    gm = rows(g_mem)
    p_scale = rows(pool_scale)
    b_pw1, b_dw, ln_g, ln_b, b_pw2 = (rows(a) for a in (cc_b_pw1, cc_b_dw, cc_ln_g, cc_ln_b, cc_b_pw2))

    mk, mv, mk_bf, mv_bf = _mem_kv(mem_prompt.reshape(bp * n_mem, d), gm, w_k, w_v, heads)
    mem_k_p = mk_bf.reshape(depth, bp, n_mem, d)
    mem_v_p = mv_bf.reshape(depth, bp, n_mem, d)
    mem_k_s = cache_mem_k.reshape(depth, -1, n_mem, d)
    mem_v_s = cache_mem_v.reshape(depth, -1, n_mem, d)

    def trunk(x, pos0, pool_hist, sconv_hist, cconv_hist, mem_k, mem_v):
        b, t, _ = x.shape
        new_pool, new_sconv, new_cconv = [], [], []
        for i in range(depth):
            x = _ffn(x.reshape(b * t, d), gp, gq, w_gu, w_down, i, 0, 4 * i).reshape(b, t, d)
            kind, slot = i % N_MIXERS, i // N_MIXERS
            if kind == 0:
                x, st = _pool_mixer(x, pool_hist[slot], gp, gq, w_pool, p_scale, slot, 4 * i + 1, pos0)
                new_pool.append(st)
            elif kind == 1:
                x, st = _sconv_mixer(x, sconv_hist[slot], gp, gq, w_sc_in, sc_w_conv, w_sc_out,
                                     slot, 4 * i + 1)
                new_sconv.append(st)
            else:
                x, st = _cconv_mixer(x, cconv_hist[slot], gp, gq, w_pw1, b_pw1, cc_w_dw, b_dw,
                                     ln_g, ln_b, w_pw2, b_pw2, slot, 4 * i + 1)
                new_cconv.append(st)
            x = _xattn(x, mem_k, mem_v, gp, gq, w_q, w_o, heads, i, 4 * i + 2)
            x = _ffn(x.reshape(b * t, d), gp, gq, w_gu, w_down, i, 1, 4 * i + 3).reshape(b, t, d)
        return x, jnp.stack(new_pool), jnp.stack(new_sconv), jnp.stack(new_cconv)

    dt = x_prompt.dtype
    n_pool, n_sconv, n_cconv = cache_pool.shape[0], cache_sconv.shape[0], cache_cconv.shape[0]
    pool_h0 = [jnp.zeros((bp, POOL_HIST, d), dt)] * n_pool
    sconv_h0 = [jnp.zeros((bp, cache_sconv.shape[2], d), dt)] * n_sconv
    cconv_h0 = [jnp.zeros((bp, cache_cconv.shape[2], d), dt)] * n_cconv
    y_prompt, pool_p, sconv_p, cconv_p = trunk(x_prompt, 0, pool_h0, sconv_h0, cconv_h0, mem_k_p, mem_v_p)
    y_sample, pool_s, sconv_s, cconv_s = trunk(
        x_sample, PAST_LEN,
        [cache_pool[j] for j in range(n_pool)],
        [cache_sconv[j] for j in range(n_sconv)],
        [cache_cconv[j] for j in range(n_cconv)],
        mem_k_s, mem_v_s)

    mem_k_prompt = mk.reshape(depth, bp, n_mem, heads, head_dim)
    mem_v_prompt = mv.reshape(depth, bp, n_mem, heads, head_dim)
    return (y_prompt, y_sample, pool_p, pool_s, sconv_p, sconv_s, cconv_p, cconv_s,
            mem_k_prompt, mem_v_prompt)
```

```python
import functools

import jax
import jax.numpy as jnp
from jax import lax
from jax.experimental import pallas as pl
from jax.experimental.pallas import tpu as pltpu

EPS = 1e-6
MACARON = 0.5
PAST_LEN = 1024
POOL_WINDOWS = (2, 4, 8, 16)
POOL_HIST = max(POOL_WINDOWS) - 1
N_MIXERS = 3

BF16 = jnp.bfloat16
F32 = jnp.float32

V7X_VMEM_BYTES = 64 * 1024 * 1024
SUBLANES = 8

assert all(w & (w - 1) == 0 for w in POOL_WINDOWS)
POOL_PAD = SUBLANES * (max(POOL_WINDOWS).bit_length() - 1)
assert POOL_PAD >= POOL_HIST + 1

ROW_TILE = 512
FFN_ROW_TILE = 1024
FFN_SUB_ROWS = 512
XATTN_ROW_TILE = 1024
XATTN_SUB_ROWS = 512
XATTN_HEADS_PER_STEP = 2
XATTN_SHORT_ROWS = 256
COL_CHUNK = 512


def _vmem_limit(block_bytes, scratch_bytes, temp_bytes):
    est = 2 * block_bytes + scratch_bytes + temp_bytes
    return int(min(est + est // 8, V7X_VMEM_BYTES - 6 * 1024 * 1024))


def _nbytes(shape, dtype):
    n = 1
    for s in shape:
        n *= s
    return n * jnp.dtype(dtype).itemsize


def _rms(x, g):
    ms = jnp.mean(x * x, axis=-1, keepdims=True)
    return x * lax.rsqrt(ms + EPS) * g


def _dot(a, b):
    return jnp.dot(a, b, preferred_element_type=F32)


def _hist_pad(h):
    return -(-h // SUBLANES) * SUBLANES


def _post_residual(o_ref, x_ref, g_ref, scale, idx=Ellipsis):
    o = o_ref[idx]
    ms = jnp.mean(o * o, axis=-1, keepdims=True)
    o_ref[idx] = x_ref[idx] + o_ref[idx] * (scale * lax.rsqrt(ms + EPS)) * g_ref[...]


def _causal_depthwise(load, w, tt, start0):
    width = w.shape[0]
    out = None
    for s in range(SUBLANES):
        taps = [k for k in range(width) if (start0 + k) % SUBLANES == s]
        if not taps:
            continue
        rows = tt if s == 0 else tt + SUBLANES
        part = None
        for k in taps:
            off = start0 + k - s
            term = w[k:k + 1, :] * load(off, rows)
            part = term if part is None else part + term
        if s:
            part = part[:, s:s + tt, :]
        out = part if out is None else out + part
    return out


def _chunk_variants(c, nc, body):
    if nc == 1:
        body(True, True)
        return
    pl.when(c == 0)(lambda: body(True, False))
    if nc > 2:
        pl.when(jnp.logical_and(c > 0, c < nc - 1))(lambda: body(False, False))
    pl.when(c == nc - 1)(lambda: body(False, True))


def _ffn_kernel(x_ref, gpre_ref, gpost_ref, wg_ref, wu_ref, wd_ref, o_ref, h_ref, *, nj):
    j = pl.program_id(1)

    tm, d = x_ref.shape
    sub = min(tm, FFN_SUB_ROWS)

    def body(first, last):
        for r in range(tm // sub):
            rows = slice(r * sub, (r + 1) * sub)
            if first:
                h_ref[rows, :] = _rms(x_ref[rows, :], gpre_ref[...]).astype(BF16)
            h = h_ref[rows, :]
            gate = _dot(h, wg_ref[...])
            up = _dot(h, wu_ref[...])
            a = (gate * jax.nn.sigmoid(gate) * up).astype(BF16)
            if first:
                o_ref[rows, :] = _dot(a, wd_ref[...])
            else:
                o_ref[rows, :] += _dot(a, wd_ref[...])
            if last:
                _post_residual(o_ref, x_ref, gpost_ref, MACARON, (rows, slice(None)))

    _chunk_variants(j, nj, body)


def _ffn(x, g_pre, g_post, w_gu, w_down, layer, slot, gidx):
    n, d = x.shape
    dff = w_down.shape[2]
    tm = min(FFN_ROW_TILE, n)
    tf = min(COL_CHUNK, dff)
    nj = dff // tf
    assert n % tm == 0 and dff % tf == 0
    blocks = (_nbytes((tm, d), F32) * 2 + _nbytes((d, tf), BF16) * 3)
    scratch = _nbytes((tm, d), BF16)
    temps = _nbytes((min(tm, FFN_SUB_ROWS), tf), F32) * 6
    return pl.pallas_call(
        functools.partial(_ffn_kernel, nj=nj),
        grid=(n // tm, nj),
        in_specs=[
            pl.BlockSpec((tm, d), lambda i, j: (i, 0)),
            pl.BlockSpec((None, 1, d), lambda i, j: (gidx, 0, 0)),
            pl.BlockSpec((None, 1, d), lambda i, j: (gidx, 0, 0)),
            pl.BlockSpec((None, None, d, tf), lambda i, j: (layer, slot, 0, j)),
            pl.BlockSpec((None, None, d, tf), lambda i, j: (layer, slot, 0, nj + j)),
            pl.BlockSpec((None, None, tf, d), lambda i, j: (layer, slot, j, 0)),
        ],
        out_specs=pl.BlockSpec((tm, d), lambda i, j: (i, 0)),
        out_shape=jax.ShapeDtypeStruct((n, d), F32),
        scratch_shapes=[pltpu.VMEM((tm, d), BF16)],
        compiler_params=pltpu.CompilerParams(
            dimension_semantics=("parallel", "arbitrary"),
            vmem_limit_bytes=_vmem_limit(blocks, scratch, temps)),
        name="ffn",
    )(x, g_pre, g_post, w_gu, w_gu, w_down)


def _seq_tiles(b, t, rows=None):
    rows = ROW_TILE if rows is None else rows
    if t >= rows:
        assert t % rows == 0
        return 1, rows
    nb = min(b, max(1, rows // t))
    assert b % nb == 0 and t % SUBLANES == 0
    return nb, t


def _pool_kernel(x_ref, hist_ref, gpre_ref, gpost_ref, w_ref, scale_ref,
                 o_ref, nh_ref, ext_ref, *, pos0):
    t = pl.program_id(1)
    nt = pl.num_programs(1)
    nb, tt, d = x_ref.shape
    hp = POOL_PAD
    gc = d // len(POOL_WINDOWS)

    @pl.when(t == 0)
    def _():
        ext_ref[:, 0:hp, :] = jnp.zeros((nb, hp, d), F32)
        ext_ref[:, hp - POOL_HIST:hp, :] = hist_ref[...]

    @pl.when(t > 0)
    def _():
        ext_ref[:, hp - POOL_HIST:hp, :] = ext_ref[:, hp + tt - POOL_HIST:hp + tt, :]

    x = x_ref[...]
    ext_ref[:, hp:hp + tt, :] = _rms(x, gpre_ref[...])
    pos = pos0 + t * tt + lax.broadcasted_iota(jnp.int32, (1, tt, 1), 1)
    span_rows = tt + hp - SUBLANES
    ys = []
    for g, w in enumerate(POOL_WINDOWS):
        sl = slice(g * gc, (g + 1) * gc)
        u = ext_ref[:, hp:hp + tt, sl]
        win = (ext_ref[:, SUBLANES:SUBLANES + span_rows, sl]
               + ext_ref[:, SUBLANES - 1:SUBLANES - 1 + span_rows, sl])
        span, origin = 2, 0
        while span < w:
            n = win.shape[1]
            win = win[:, SUBLANES:, :] + win[:, SUBLANES - span:n - span, :]
            span, origin = 2 * span, origin + SUBLANES
        first = hp - SUBLANES - origin
        win = win[:, first:first + tt, :]
        inv_cnt = 1.0 / jnp.minimum(pos + 1, w).astype(F32)
        dg = (win * inv_cnt - u).reshape(nb * tt, gc).astype(BF16)
        ys.append(_dot(dg, w_ref[g]))
    y = jnp.concatenate(ys, axis=-1) * scale_ref[...]
    out = x.reshape(nb * tt, d) + _rms(y, gpost_ref[...])
    o_ref[...] = out.reshape(nb, tt, d)

    @pl.when(t == nt - 1)
    def _():
        nh_ref[...] = ext_ref[:, hp + tt - POOL_HIST:hp + tt, :]


def _pool_mixer(x, hist, g_pre, g_post, pool_w, pool_scale, slot, gidx, pos0):
    b, t, d = x.shape
    nb, tt = _seq_tiles(b, t)
    assert tt >= POOL_HIST
    ng = len(POOL_WINDOWS)
    gc = d // ng
    hp = POOL_PAD
    blocks = _nbytes((nb, tt, d), F32) * 2 + _nbytes((ng, gc, gc), BF16)
    scratch = _nbytes((nb, hp + tt, d), F32)
    temps = _nbytes((nb, tt, d), F32) * 4
    return pl.pallas_call(
        functools.partial(_pool_kernel, pos0=pos0),
        grid=(b // nb, t // tt),
        in_specs=[
            pl.BlockSpec((nb, tt, d), lambda i, j: (i, j, 0)),
            pl.BlockSpec((nb, POOL_HIST, d), lambda i, j: (i, 0, 0)),
            pl.BlockSpec((None, 1, d), lambda i, j: (gidx, 0, 0)),
            pl.BlockSpec((None, 1, d), lambda i, j: (gidx, 0, 0)),
            pl.BlockSpec((None, ng, gc, gc), lambda i, j: (slot, 0, 0, 0)),
            pl.BlockSpec((None, 1, d), lambda i, j: (slot, 0, 0)),
        ],
        out_specs=[
            pl.BlockSpec((nb, tt, d), lambda i, j: (i, j, 0)),
            pl.BlockSpec((nb, POOL_HIST, d), lambda i, j: (i, 0, 0)),
        ],
        out_shape=[jax.ShapeDtypeStruct((b, t, d), F32),
                   jax.ShapeDtypeStruct((b, POOL_HIST, d), F32)],
        scratch_shapes=[pltpu.VMEM((nb, hp + tt, d), F32)],
        compiler_params=pltpu.CompilerParams(
            dimension_semantics=("parallel", "arbitrary"),
            vmem_limit_bytes=_vmem_limit(blocks, scratch, temps)),
        name="pool_mixer",
    )(x, hist, g_pre, g_post, pool_w, pool_scale)


def _sconv_kernel(x_ref, hist_ref, gpre_ref, gpost_ref, wb_ref, wc_ref, wh_ref,
                  wconv_ref, wout_ref, o_ref, nh_ref, u_ref, carry_ref, ext_ref, *, nc):
    t = pl.program_id(1)
    nt = pl.num_programs(1)
    c = pl.program_id(2)
    nb, tt, d = x_ref.shape
    tn = wb_ref.shape[-1]
    width = wconv_ref.shape[0]
    nh = width - 1
    hp = _hist_pad(nh)

    @pl.when(t == 0)
    def _():
        ext_ref[:, 0:hp, :] = jnp.zeros((nb, hp, tn), F32)
        ext_ref[:, hp - nh:hp, :] = hist_ref[...]

    @pl.when(t > 0)
    def _():
        ext_ref[:, hp - nh:hp, :] = carry_ref[c, :, 0:nh, :]

    def body(first, last):
        if first:
            u_ref[...] = _rms(x_ref[...], gpre_ref[...]).reshape(nb * tt, d).astype(BF16)
        u = u_ref[...]
        gb = _dot(u, wb_ref[...])
        v = (_dot(u, wc_ref[...]) * _dot(u, wh_ref[...])).reshape(nb, tt, tn)
        ext_ref[:, hp:hp + tt, :] = v
        carry_ref[c, :, 0:nh, :] = v[:, tt - nh:tt, :]
        conv = _causal_depthwise(lambda off, rows: ext_ref[:, off:off + rows, :],
                                 wconv_ref[...], tt, hp - nh)
        y =(gb.reshape(nb, tt, tn) * conv).reshape(nb * tt, tn).astype(BF16)
        if first:
            o_ref[...] = _dot(y, wout_ref[...]).reshape(nb, tt, d)
        else:
            o_ref[...] += _dot(y, wout_ref[...]).reshape(nb, tt, d)
        if last:
            _post_residual(o_ref, x_ref, gpost_ref, 1.0)

    _chunk_variants(c, nc, body)

    @pl.when(jnp.logical_and(t == nt - 1, c == nc - 1))
    def _():
        for k in range(nc):
            nh_ref[:, :, k * tn:(k + 1) * tn] = carry_ref[k, :, 0:nh, :]


def _sconv_mixer(x, hist, g_pre, g_post, w_in, w_conv, w_out, slot, gidx):
    b, t, d = x.shape
    nb, tt = _seq_tiles(b, t)
    width = w_conv.shape[1]
    nh = width - 1
    assert tt >= nh
    tn = min(COL_CHUNK, d)
    nc = d // tn
    hp = _hist_pad(nh)
    blocks = _nbytes((nb, tt, d), F32) * 2 + _nbytes((d, tn), BF16) * 4
    scratch = (_nbytes((nb * tt, d), BF16) + _nbytes((nc, nb, SUBLANES, tn), F32)
               + _nbytes((nb, hp + tt, tn), F32))
    temps = _nbytes((nb * tt, tn), F32) * 6 + _nbytes((nb * tt, d), F32)
    return pl.pallas_call(
        functools.partial(_sconv_kernel, nc=nc),
        grid=(b // nb, t // tt, nc),
        in_specs=[
            pl.BlockSpec((nb, tt, d), lambda i, j, c: (i, j, 0)),
            pl.BlockSpec((nb, nh, tn), lambda i, j, c: (i, 0, c)),
            pl.BlockSpec((None, 1, d), lambda i, j, c: (gidx, 0, 0)),
            pl.BlockSpec((None, 1, d), lambda i, j, c: (gidx, 0, 0)),
            pl.BlockSpec((None, d, tn), lambda i, j, c: (slot, 0, c)),
            pl.BlockSpec((None, d, tn), lambda i, j, c: (slot, 0, nc + c)),
            pl.BlockSpec((None, d, tn), lambda i, j, c: (slot, 0, 2 * nc + c)),
            pl.BlockSpec((None, width, tn), lambda i, j, c: (slot, 0, c)),
            pl.BlockSpec((None, tn, d), lambda i, j, c: (slot, c, 0)),
        ],
        out_specs=[
            pl.BlockSpec((nb, tt, d), lambda i, j, c: (i, j, 0)),
            pl.BlockSpec((nb, nh, d), lambda i, j, c: (i, 0, 0)),
        ],
        out_shape=[jax.ShapeDtypeStruct((b, t, d), F32),
                   jax.ShapeDtypeStruct((b, nh, d), F32)],
        scratch_shapes=[pltpu.VMEM((nb * tt, d), BF16),
                        pltpu.VMEM((nc, nb, SUBLANES, tn), F32),
                        pltpu.VMEM((nb, hp + tt, tn), F32)],
        compiler_params=pltpu.CompilerParams(
            dimension_semantics=("parallel", "arbitrary", "arbitrary"),
            vmem_limit_bytes=_vmem_limit(blocks, scratch, temps)),
        name="sconv_mixer",
    )(x, hist, g_pre, g_post, w_in, w_in, w_in, w_conv, w_out)


def _cconv_kernel(x_ref, hist_ref, gpre_ref, gpost_ref, wa_ref, wg_ref, ba_ref, bg_ref,
                  wdw_ref, bdw_ref, lng_ref, lnb_ref, w2_ref, b2_ref,
                  o_ref, nh_ref, u_ref, hc_ref, carry_ref, ext_ref, *, nc):
    t = pl.program_id(1)
    nt = pl.num_programs(1)
    c = pl.program_id(2)
    nb, tt, d = x_ref.shape
    tn = wa_ref.shape[-1]
    width = wdw_ref.shape[0]
    nh = width - 1
    hp = _hist_pad(nh)
    m = nb * tt

    @pl.when(jnp.logical_and(c < nc, t == 0))
    def _():
        ext_ref[:, 0:hp, :] = jnp.zeros((nb, hp, tn), F32)
        ext_ref[:, hp - nh:hp, :] = hist_ref[...]

    @pl.when(jnp.logical_and(c < nc, t > 0))
    def _():
        ext_ref[:, hp - nh:hp, :] = carry_ref[c, :, 0:nh, :]

    def project_and_conv(first):
        if first:
            u_ref[...] = _rms(x_ref[...], gpre_ref[...]).reshape(m, d).astype(BF16)
        u = u_ref[...]
        a = _dot(u, wa_ref[...]) + ba_ref[...]
        gate = _dot(u, wg_ref[...]) + bg_ref[...]
        v = (a * jax.nn.sigmoid(gate)).reshape(nb, tt, tn)
        ext_ref[:, hp:hp + tt, :] = v
        carry_ref[c, :, 0:nh, :] = v[:, tt - nh:tt, :]
        conv = _causal_depthwise(lambda off, rows: ext_ref[:, off:off + rows, :],
                                 wdw_ref[...], tt, hp - nh) + bdw_ref[...]
        hc_ref[c] = conv.reshape(m, tn)

    pl.when(c == 0)(lambda: project_and_conv(True))
    if nc > 1:
        pl.when(jnp.logical_and(c > 0, c < nc))(lambda: project_and_conv(False))

    def layernorm_silu():
        s1 = jnp.zeros((m, 1), F32)
        for k in range(nc):
            s1 = s1 + jnp.sum(hc_ref[k], axis=-1, keepdims=True)
        mean = s1 / d
        s2 = jnp.zeros((m, 1), F32)
        for k in range(nc):
            xc = hc_ref[k] - mean
            s2 = s2 + jnp.sum(xc * xc, axis=-1, keepdims=True)
        rstd = lax.rsqrt(s2 / d + EPS)
        for k in range(nc):
            sl = slice(k * tn, (k + 1) * tn)
            hn = (hc_ref[k] - mean) * rstd * lng_ref[:, sl] + lnb_ref[:, sl]
            u_ref[:, sl] = (hn * jax.nn.sigmoid(hn)).astype(BF16)

    for k in range(nc):
        @pl.when(c == nc + k)
        def _(k=k):
            if k == 0:
                layernorm_silu()
            sl = slice(k * tn, (k + 1) * tn)
            y = _dot(u_ref[...], w2_ref[...]) + b2_ref[:, sl]
            o_ref[:, :, sl] = y.reshape(nb, tt, tn)
            if k == nc - 1:
                _post_residual(o_ref, x_ref, gpost_ref, 1.0)

    @pl.when(jnp.logical_and(t == nt - 1, c == 2 * nc - 1))
    def _():
        for k in range(nc):
            nh_ref[:, :, k * tn:(k + 1) * tn] = carry_ref[k, :, 0:nh, :]


def _cconv_mixer(x, hist, g_pre, g_post, w_pw1, b_pw1, w_dw, b_dw, ln_g, ln_b,
                 w_pw2, b_pw2, slot, gidx):
    b, t, d = x.shape
    nb, tt = _seq_tiles(b, t)
    width = w_dw.shape[1]
    nh = width - 1
    assert tt >= nh
    tn = min(COL_CHUNK, d)
    nc = d // tn
    hp = _hist_pad(nh)
    m = nb * tt

    def p1(c):
        return jnp.minimum(c, nc - 1)

    def p2(c):
        return jnp.maximum(c - nc, 0)

    blocks = (_nbytes((nb, tt, d), F32) * 2 + _nbytes((d, tn), BF16) * 3
              + _nbytes((nb, nh, tn), F32) * 2)
    scratch = (_nbytes((m, d), BF16) + _nbytes((nc, m, tn), F32)
               + _nbytes((nc, nb, hp, tn), F32) + _nbytes((nb, hp + tt, tn), F32))
    temps = _nbytes((m, tn), F32) * 6 + _nbytes((m, d), F32)
    return pl.pallas_call(
        functools.partial(_cconv_kernel, nc=nc),
        grid=(b // nb, t // tt, 2 * nc),
        in_specs=[
            pl.BlockSpec((nb, tt, d), lambda i, j, c: (i, j, 0)),
            pl.BlockSpec((nb, nh, tn), lambda i, j, c: (i, 0, p1(c))),
            pl.BlockSpec((None, 1, d), lambda i, j, c: (gidx, 0, 0)),
            pl.BlockSpec((None, 1, d), lambda i, j, c: (gidx, 0, 0)),
            pl.BlockSpec((None, d, tn), lambda i, j, c: (slot, 0, p1(c))),
            pl.BlockSpec((None, d, tn), lambda i, j, c: (slot, 0, nc + p1(c))),
            pl.BlockSpec((None, 1, tn), lambda i, j, c: (slot, 0, p1(c))),
            pl.BlockSpec((None, 1, tn), lambda i, j, c: (slot, 0, nc + p1(c))),
            pl.BlockSpec((None, width, tn), lambda i, j, c: (slot, 0, p1(c))),
            pl.BlockSpec((None, 1, tn), lambda i, j, c: (slot, 0, p1(c))),
            pl.BlockSpec((None, 1, d), lambda i, j, c: (slot, 0, 0)),
            pl.BlockSpec((None, 1, d), lambda i, j, c: (slot, 0, 0)),
            pl.BlockSpec((None, d, tn), lambda i, j, c: (slot, 0, p2(c))),
            pl.BlockSpec((None, 1, d), lambda i, j, c: (slot, 0, 0)),
        ],
        out_specs=[
            pl.BlockSpec((nb, tt, d), lambda i, j, c: (i, j, 0)),
            pl.BlockSpec((nb, nh, d), lambda i, j, c: (i, 0, 0)),
        ],
        out_shape=[jax.ShapeDtypeStruct((b, t, d), F32),
                   jax.ShapeDtypeStruct((b, nh, d), F32)],
        scratch_shapes=[pltpu.VMEM((m, d), BF16),
                        pltpu.VMEM((nc, m, tn), F32),
                        pltpu.VMEM((nc, nb, hp, tn), F32),
                        pltpu.VMEM((nb, hp + tt, tn), F32)],
        compiler_params=pltpu.CompilerParams(
            dimension_semantics=("parallel", "arbitrary", "arbitrary"),
            vmem_limit_bytes=_vmem_limit(blocks, scratch, temps)),
        name="cconv_mixer",
    )(x, hist, g_pre, g_post, w_pw1, w_pw1, b_pw1, b_pw1, w_dw, b_dw, ln_g, ln_b,
      w_pw2, b_pw2)


def _xattn_kernel(x_ref, k_ref, v_ref, gpre_ref, gpost_ref, wq_ref, wo_ref,
                  o_ref, u_ref, *, nsteps, hd, tt):
    step = pl.program_id(2)
    m, d = x_ref.shape
    nb = k_ref.shape[0]
    heads_here = wq_ref.shape[-1] // hd
    sub = min(m, XATTN_SUB_ROWS)
    seq_rows = min(sub, tt)

    def body(first, last):
        for r in range(m // sub):
            rows = slice(r * sub, (r + 1) * sub)
            if first:
                u_ref[rows, :] = _rms(x_ref[rows, :], gpre_ref[...]).astype(BF16)
            q = _dot(u_ref[rows, :], wq_ref[...]).astype(BF16)
            outs = []
            for s0 in range(0, sub, seq_rows):
                b = (r * sub + s0) // tt if nb > 1 else 0
                per_head = []
                for hh in range(heads_here):
                    cols = slice(hh * hd, (hh + 1) * hd)
                    kb = k_ref[b, :, cols].astype(BF16)
                    vb = v_ref[b, :, cols].astype(BF16)
                    s = lax.dot_general(q[s0:s0 + seq_rows, cols], kb, (((1,), (1,)), ((), ())),
                                        preferred_element_type=F32) * (hd ** -0.5)
                    e = jnp.exp(s - jnp.max(s, axis=-1, keepdims=True))
                    p = (e / jnp.sum(e, axis=-1, keepdims=True)).astype(BF16)
                    per_head.append(_dot(p, vb).astype(BF16))
                outs.append(per_head[0] if heads_here == 1 else jnp.concatenate(per_head, axis=1))
            o = outs[0] if len(outs) == 1 else jnp.concatenate(outs, axis=0)
            if first:
                o_ref[rows, :] = _dot(o, wo_ref[...])
            else:
                o_ref[rows, :] += _dot(o, wo_ref[...])
            if last:
                _post_residual(o_ref, x_ref, gpost_ref, 1.0, (rows, slice(None)))

    _chunk_variants(step, nsteps, body)


def _xattn(x, mem_k, mem_v, g_pre, g_post, w_q, w_o, nheads, layer, gidx):
    b, t, d = x.shape
    long_seq = t >= XATTN_ROW_TILE
    nb, tt = _seq_tiles(b, t, XATTN_ROW_TILE if long_seq else XATTN_SHORT_ROWS)
    n_mem = mem_k.shape[2]
    hd = d // nheads
    hs = XATTN_HEADS_PER_STEP if long_seq and nheads % XATTN_HEADS_PER_STEP == 0 else 1
    cw = hs * hd
    nsteps = nheads // hs
    m = nb * tt
    nt = t // tt
    sub = min(m, XATTN_SUB_ROWS)
    assert m % sub == 0 and (sub % tt == 0 or tt % sub == 0)
    blocks = (_nbytes((m, d), F32) * 2 + _nbytes((nb, n_mem, cw), mem_k.dtype) * 2
              + _nbytes((d, cw), BF16) * 2)
    scratch = _nbytes((m, d), BF16)
    temps = _nbytes((sub, cw), F32) * 4 + _nbytes((sub, n_mem), F32) * 4 * hs
    out = pl.pallas_call(
        functools.partial(_xattn_kernel, nsteps=nsteps, hd=hd, tt=tt),
        grid=(b // nb, nt, nsteps),
        in_specs=[
            pl.BlockSpec((m, d), lambda i, j, h: (i * nt + j, 0)),
            pl.BlockSpec((None, nb, n_mem, cw), lambda i, j, h: (layer, i, 0, h)),
            pl.BlockSpec((None, nb, n_mem, cw), lambda i, j, h: (layer, i, 0, h)),
            pl.BlockSpec((None, 1, d), lambda i, j, h: (gidx, 0, 0)),
            pl.BlockSpec((None, 1, d), lambda i, j, h: (gidx, 0, 0)),
            pl.BlockSpec((None, d, cw), lambda i, j, h: (layer, 0, h)),
            pl.BlockSpec((None, cw, d), lambda i, j, h: (layer, h, 0)),
        ],
        out_specs=pl.BlockSpec((m, d), lambda i, j, h: (i * nt + j, 0)),
        out_shape=jax.ShapeDtypeStruct((b * t, d), F32),
        scratch_shapes=[pltpu.VMEM((m, d), BF16)],
        compiler_params=pltpu.CompilerParams(
            dimension_semantics=("parallel", "arbitrary", "arbitrary"),
            vmem_limit_bytes=_vmem_limit(blocks, scratch, temps)),
        name="xattn",
    )(x.reshape(b * t, d), mem_k, mem_v, g_pre, g_post, w_q, w_o)
    return out.reshape(b, t, d)


def _memkv_kernel(mem_ref, g_ref, wk_ref, wv_ref, k_ref, v_ref, kb_ref, vb_ref, m_ref,
                  *, nheads):
    c = pl.program_id(2)

    @pl.when(c == 0)
    def _():
        m_ref[...] = _rms(mem_ref[...], g_ref[...]).astype(BF16)

    mm = m_ref[...]
    k = _dot(mm, wk_ref[...])
    v = _dot(mm, wv_ref[...])
    kb_ref[...] = k.astype(BF16)
    vb_ref[...] = v.astype(BF16)
    for hh in range(nheads):
        @pl.when(c == hh)
        def _(hh=hh):
            k_ref[:, hh, :] = k
            v_ref[:, hh, :] = v


def _mem_kv(mem, g_mem, w_k, w_v, nheads):
    n, d = mem.shape
    depth = w_k.shape[0]
    hd = d // nheads
    tm = min(ROW_TILE, n)
    assert n % tm == 0
    padded_heads = -(-nheads // SUBLANES) * SUBLANES
    blocks = (_nbytes((tm, d), F32) + _nbytes((d, hd), BF16) * 2
              + _nbytes((tm, padded_heads, hd), F32) * 2 + _nbytes((tm, hd), BF16) * 2)
    scratch = _nbytes((tm, d), BF16)
    temps = _nbytes((tm, hd), F32) * 4
    out = jax.ShapeDtypeStruct((depth, n, nheads, hd), F32)
    out_bf = jax.ShapeDtypeStruct((depth, n, d), BF16)
    return pl.pallas_call(
        functools.partial(_memkv_kernel, nheads=nheads),
        grid=(depth, n // tm, nheads),
        in_specs=[
            pl.BlockSpec((tm, d), lambda l, i, c: (i, 0)),
            pl.BlockSpec((None, 1, d), lambda l, i, c: (l, 0, 0)),
            pl.BlockSpec((None, d, hd), lambda l, i, c: (l, 0, c)),
            pl.BlockSpec((None, d, hd), lambda l, i, c: (l, 0, c)),
        ],
        out_specs=[pl.BlockSpec((None, tm, nheads, hd), lambda l, i, c: (l, i, 0, 0)),
                   pl.BlockSpec((None, tm, nheads, hd), lambda l, i, c: (l, i, 0, 0)),
                   pl.BlockSpec((None, tm, hd), lambda l, i, c: (l, i, c)),
                   pl.BlockSpec((None, tm, hd), lambda l, i, c: (l, i, c))],
        out_shape=[out, out, out_bf, out_bf],
        scratch_shapes=[pltpu.VMEM((tm, d), BF16)],
        compiler_params=pltpu.CompilerParams(
            dimension_semantics=("parallel", "parallel", "arbitrary"),
            vmem_limit_bytes=_vmem_limit(blocks, scratch, temps)),
        name="mem_kv",
    )(mem, g_mem, w_k, w_v)


def kernel(x_prompt, x_sample, mem_prompt, cache_pool, cache_sconv, cache_cconv, cache_mem_k, cache_mem_v, g_pre, g_post, g_mem, ffn_w_gu, ffn_w_down, xa_w_q, xa_w_k, xa_w_v, xa_w_o, pool_w, pool_scale, sc_w_in, sc_w_conv, sc_w_out, cc_w_pw1, cc_b_pw1, cc_w_dw, cc_b_dw, cc_ln_g, cc_ln_b, cc_w_pw2, cc_b_pw2):
    depth = g_pre.shape[0]
    d = x_prompt.shape[-1]
    bp = x_prompt.shape[0]
    n_mem = mem_prompt.shape[1]
    heads, head_dim = cache_mem_k.shape[3], cache_mem_k.shape[4]

    w_gu, w_down = ffn_w_gu.astype(BF16), ffn_w_down.astype(BF16)
    w_q, w_k, w_v, w_o = (w.astype(BF16) for w in (xa_w_q, xa_w_k, xa_w_v, xa_w_o))
    w_pool = pool_w.astype(BF16)
    w_sc_in, w_sc_out = sc_w_in.astype(BF16), sc_w_out.astype(BF16)
    w_pw1, w_pw2 = cc_w_pw1.astype(BF16), cc_w_pw2.astype(BF16)

    def rows(a):
        return a.reshape(-1, 1, a.shape[-1])

    gp, gq = rows(g_pre), rows(g_post)
    gm = rows(g_mem)
    p_scale = rows(pool_scale)
    b_pw1, b_dw, ln_g, ln_b, b_pw2 = (rows(a) for a in (cc_b_pw1, cc_b_dw, cc_ln_g, cc_ln_b, cc_b_pw2))

    mk, mv, mk_bf, mv_bf = _mem_kv(mem_prompt.reshape(bp * n_mem, d), gm, w_k, w_v, heads)
    mem_k_p = mk_bf.reshape(depth, bp, n_mem, d)
    mem_v_p = mv_bf.reshape(depth, bp, n_mem, d)
    mem_k_s = cache_mem_k.reshape(depth, -1, n_mem, d)
    mem_v_s = cache_mem_v.reshape(depth, -1, n_mem, d)

    def trunk(x, pos0, pool_hist, sconv_hist, cconv_hist, mem_k, mem_v):
        b, t, _ = x.shape
        new_pool, new_sconv, new_cconv = [], [], []
        for i in range(depth):
            x = _ffn(x.reshape(b * t, d), gp, gq, w_gu, w_down, i, 0, 4 * i).reshape(b, t, d)
            kind, slot = i % N_MIXERS, i // N_MIXERS
            if kind == 0:
                x, st = _pool_mixer(x, pool_hist[slot], gp, gq, w_pool, p_scale, slot, 4 * i + 1, pos0)
                new_pool.append(st)
            elif kind == 1:
                x, st = _sconv_mixer(x, sconv_hist[slot], gp, gq, w_sc_in, sc_w_conv, w_sc_out,
                                     slot, 4 * i + 1)
                new_sconv.append(st)
            else:
                x, st = _cconv_mixer(x, cconv_hist[slot], gp, gq, w_pw1, b_pw1, cc_w_dw, b_dw,
                                     ln_g, ln_b, w_pw2, b_pw2, slot, 4 * i + 1)
                new_cconv.append(st)
            x = _xattn(x, mem_k, mem_v, gp, gq, w_q, w_o, heads, i, 4 * i + 2)
            x = _ffn(x.reshape(b * t, d), gp, gq, w_gu, w_down, i, 1, 4 * i + 3).reshape(b, t, d)
        return x, jnp.stack(new_pool), jnp.stack(new_sconv), jnp.stack(new_cconv)

    dt = x_prompt.dtype
    n_pool, n_sconv, n_cconv = cache_pool.shape[0], cache_sconv.shape[0], cache_cconv.shape[0]
    pool_h0 = [jnp.zeros((bp, POOL_HIST, d), dt)] * n_pool
    sconv_h0 = [jnp.zeros((bp, cache_sconv.shape[2], d), dt)] * n_sconv
    cconv_h0 = [jnp.zeros((bp, cache_cconv.shape[2], d), dt)] * n_cconv
    y_prompt, pool_p, sconv_p, cconv_p = trunk(x_prompt, 0, pool_h0, sconv_h0, cconv_h0, mem_k_p, mem_v_p)
    y_sample, pool_s, sconv_s, cconv_s = trunk(
        x_sample, PAST_LEN,
        [cache_pool[j] for j in range(n_pool)],
        [cache_sconv[j] for j in range(n_sconv)],
        [cache_cconv[j] for j in range(n_cconv)],
        mem_k_s, mem_v_s)

    mem_k_prompt = mk.reshape(depth, bp, n_mem, heads, head_dim)
    mem_v_prompt = mv.reshape(depth, bp, n_mem, heads, head_dim)
    return (y_prompt, y_sample, pool_p, pool_s, sconv_p, sconv_s, cconv_p, cconv_s,
            mem_k_prompt, mem_v_prompt)
```
